```python
import jax, jax.numpy as jnp
from jax import lax
import numpy as np

D_MODEL = 2048
BATCH = 4
SEQ = 8192
DEPTH = 4

N_MIXERS = 4
MEM_LEN = 256
XATTN_HEADS = 4
XATTN_HEAD_DIM = D_MODEL // XATTN_HEADS
SHORT_CONV_WIDTH = 3
POOL_WINDOWS = (2, 4, 8, 16)
POOL_GROUPS = len(POOL_WINDOWS)
POOL_GROUP_DIM = D_MODEL // POOL_GROUPS
GMLP_CHUNK = 128
GMLP_GROUPS = 4
GMLP_WIDTH = D_MODEL
GMLP_GROUP_DIM = GMLP_WIDTH // GMLP_GROUPS
CONFORMER_CONV_WIDTH = 31
D_FF = 5632
N_EXPERTS = 8
TOP_K = 2
D_FF_EXPERT = 5632
NORM_EPS = 1e-6
N_LAYERS_A = len(range(0, DEPTH, N_MIXERS))
N_LAYERS_B = len(range(1, DEPTH, N_MIXERS))
N_LAYERS_C = len(range(2, DEPTH, N_MIXERS))
N_LAYERS_D = len(range(3, DEPTH, N_MIXERS))
N_DENSE = len(range(0, DEPTH, 2))
N_MOE = len(range(1, DEPTH, 2))

kernel_name = 'hybrid_interleaved_conv_pool_gmlp_conformer_moe'


def _rmsnorm(x, g):
    xf = x.astype(jnp.float32)
    xf = xf * lax.rsqrt(jnp.mean(xf * xf, axis=-1, keepdims=True) + NORM_EPS)
    return xf.astype(x.dtype) * g


def _layernorm(x, g, b):
    xf = x.astype(jnp.float32)
    mu = jnp.mean(xf, axis=-1, keepdims=True)
    xc = xf - mu
    xf = xc * lax.rsqrt(jnp.mean(xc * xc, axis=-1, keepdims=True) + NORM_EPS)
    return xf.astype(x.dtype) * g + b


def _causal_depthwise_conv(h, w):
    k_width, c = w.shape
    hp = jnp.pad(h, ((0, 0), (k_width - 1, 0), (0, 0)))
    return lax.conv_general_dilated(
        hp, w.astype(h.dtype)[:, None, :], window_strides=(1,), padding='VALID',
        dimension_numbers=('NWC', 'WIO', 'NWC'), feature_group_count=c)


def short_conv_mixer(h, w_in, conv_w, w_out):
    b_gate, c_gate, z = jnp.split(h @ w_in, 3, axis=-1)
    return (b_gate * _causal_depthwise_conv(c_gate * z, conv_w)) @ w_out


def pooling_mixer(h, w_map, scale):
    b_, s_, d_ = h.shape
    hf = h.astype(jnp.float32).reshape(b_, s_, POOL_GROUPS, POOL_GROUP_DIM)
    cs = jnp.cumsum(hf, axis=1)
    pos = jnp.arange(s_, dtype=jnp.int32)
    pooled = []
    for g, w in enumerate(POOL_WINDOWS):
        csg = cs[:, :, g]
        shifted = jnp.pad(csg, ((0, 0), (w, 0), (0, 0)))[:, :s_]
        count = jnp.minimum(pos + 1, w).astype(jnp.float32)[None, :, None]
        pooled.append((csg - shifted) / count - hf[:, :, g])
    p = jnp.stack(pooled, axis=2).astype(h.dtype)
    y = jnp.einsum('bsgc,gcd->bsgd', p, w_map).reshape(b_, s_, d_)
    return y * scale


def chunked_gmlp_mixer(h, w_uv, v_gain, w_s, b_s, w_out):
    b_, s_, _ = h.shape
    u, v = jnp.split(jax.nn.gelu(h @ w_uv, approximate=False), 2, axis=-1)
    v = _rmsnorm(v, v_gain)
    n_chunks = s_ // GMLP_CHUNK
    v = v.reshape(b_, n_chunks, GMLP_CHUNK, GMLP_GROUPS, GMLP_GROUP_DIM)
    mask = jnp.tril(jnp.ones((GMLP_CHUNK, GMLP_CHUNK), dtype=bool))
    ws = jnp.where(mask[None], w_s, jnp.zeros_like(w_s)).astype(v.dtype)
    sv = jnp.einsum('gts,bnsgc->bntgc', ws, v) + b_s.T[None, None, :, :, None]
    return (u * sv.reshape(b_, s_, GMLP_WIDTH)) @ w_out


def conformer_conv_mixer(h, w_pw1, conv_w, conv_b, ln_g, ln_b, w_pw2):
    a, g = jnp.split(h @ w_pw1, 2, axis=-1)
    z = a * jax.nn.sigmoid(g)
    z = _causal_depthwise_conv(z, conv_w) + conv_b
    z = jax.nn.silu(_layernorm(z, ln_g, ln_b))
    return z @ w_pw2


def cross_attention(h, memn, wq, wk, wv, wo):
    b_, s_, d_ = h.shape
    m_ = memn.shape[1]
    q = (h @ wq).reshape(b_, s_, XATTN_HEADS, XATTN_HEAD_DIM)
    k = (memn @ wk).reshape(b_, m_, XATTN_HEADS, XATTN_HEAD_DIM)
    v = (memn @ wv).reshape(b_, m_, XATTN_HEADS, XATTN_HEAD_DIM)
    s = jnp.einsum('bshd,bmhd->bhsm', q, k).astype(jnp.float32) * (XATTN_HEAD_DIM ** -0.5)
    p = jax.nn.softmax(s, axis=-1).astype(h.dtype)
    o = jnp.einsum('bhsm,bmhd->bshd', p, v).reshape(b_, s_, d_)
    return o @ wo


def swiglu(t, w1, w3, w2):
    return (jax.nn.silu(t @ w1) * (t @ w3)) @ w2


def moe_ffn(h, router, w1, w3, w2):
    b_, s_, d_ = h.shape
    t = h.reshape(-1, d_)
    logits = (t @ router).astype(jnp.float32)
    top_vals, top_idx = lax.top_k(logits, TOP_K)
    gates = jax.nn.softmax(top_vals, axis=-1)
    combine = jnp.einsum('tk,tke->te', gates,
                         jax.nn.one_hot(top_idx, N_EXPERTS, dtype=jnp.float32)).astype(h.dtype)
    out = jnp.zeros_like(t)
    for e in range(N_EXPERTS):
        out = out + combine[:, e:e + 1] * swiglu(t, w1[e], w3[e], w2[e])
    return out.reshape(b_, s_, d_)


def setup_inputs(seed: int = 0) -> dict:
    key = jax.random.key(seed)
    ks = iter(jax.random.split(key, 40))
    D = D_MODEL

    def nrm(shape, scale):
        return jax.random.normal(next(ks), shape, jnp.float32) * scale

    def gain(shape):
        return 1.0 + nrm(shape, 0.02)

    return {
        'x': nrm((BATCH, SEQ, D), 1.0),
        'mem': nrm((BATCH, MEM_LEN, D), 1.0),
        'mem_norm': gain((D,)),
        'final_norm': gain((D,)),
        'mixer_norm': gain((DEPTH, D)),
        'xattn_norm': gain((DEPTH, D)),
        'ffn_norm': gain((DEPTH, D)),
        'xa_wq': nrm((DEPTH, D, D), D ** -0.5),
        'xa_wk': nrm((DEPTH, D, D), D ** -0.5),
        'xa_wv': nrm((DEPTH, D, D), D ** -0.5),
        'xa_wo': nrm((DEPTH, D, D), D ** -0.5),
        'a_w_in': nrm((N_LAYERS_A, D, 3 * D), D ** -0.5),
        'a_conv': nrm((N_LAYERS_A, SHORT_CONV_WIDTH, D), SHORT_CONV_WIDTH ** -0.5),
        'a_w_out': nrm((N_LAYERS_A, D, D), D ** -0.5),
        'b_w_map': nrm((N_LAYERS_B, POOL_GROUPS, POOL_GROUP_DIM, POOL_GROUP_DIM), POOL_GROUP_DIM ** -0.5),
        'b_scale': gain((N_LAYERS_B, D)),
        'c_w_uv': nrm((N_LAYERS_C, D, 2 * GMLP_WIDTH), D ** -0.5),
        'c_v_norm': gain((N_LAYERS_C, GMLP_WIDTH)),
        'c_w_s': nrm((N_LAYERS_C, GMLP_GROUPS, GMLP_CHUNK, GMLP_CHUNK), GMLP_CHUNK ** -0.5),
        'c_b_s': gain((N_LAYERS_C, GMLP_GROUPS, GMLP_CHUNK)),
        'c_w_out': nrm((N_LAYERS_C, GMLP_WIDTH, D), GMLP_WIDTH ** -0.5),
        'd_w_pw1': nrm((N_LAYERS_D, D, 2 * D), D ** -0.5),
        'd_conv': nrm((N_LAYERS_D, CONFORMER_CONV_WIDTH, D), CONFORMER_CONV_WIDTH ** -0.5),
        'd_conv_b': nrm((N_LAYERS_D, D), 0.02),
        'd_ln_g': gain((N_LAYERS_D, D)),
        'd_ln_b': nrm((N_LAYERS_D, D), 0.02),
        'd_w_pw2': nrm((N_LAYERS_D, D, D), D ** -0.5),
        'f_w1': nrm((N_DENSE, D, D_FF), D ** -0.5),
        'f_w3': nrm((N_DENSE, D, D_FF), D ** -0.5),
        'f_w2': nrm((N_DENSE, D_FF, D), D_FF ** -0.5),
        'm_router': nrm((N_MOE, D, N_EXPERTS), D ** -0.5),
        'm_w1': nrm((N_MOE, N_EXPERTS, D, D_FF_EXPERT), D ** -0.5),
        'm_w3': nrm((N_MOE, N_EXPERTS, D, D_FF_EXPERT), D ** -0.5),
        'm_w2': nrm((N_MOE, N_EXPERTS, D_FF_EXPERT, D), D_FF_EXPERT ** -0.5),
    }


def reference(x, mem, mem_norm, final_norm, mixer_norm, xattn_norm, ffn_norm,
              xa_wq, xa_wk, xa_wv, xa_wo,
              a_w_in, a_conv, a_w_out,
              b_w_map, b_scale,
              c_w_uv, c_v_norm, c_w_s, c_b_s, c_w_out,
              d_w_pw1, d_conv, d_conv_b, d_ln_g, d_ln_b, d_w_pw2,
              f_w1, f_w3, f_w2,
              m_router, m_w1, m_w3, m_w2):
    memn = _rmsnorm(mem, mem_norm)
    for i in range(DEPTH):
        mixer = i % N_MIXERS
        j = i // N_MIXERS
        h = _rmsnorm(x, mixer_norm[i])
        if mixer == 0:
            y = short_conv_mixer(h, a_w_in[j], a_conv[j], a_w_out[j])
        elif mixer == 1:
            y = pooling_mixer(h, b_w_map[j], b_scale[j])
        elif mixer == 2:
            y = chunked_gmlp_mixer(h, c_w_uv[j], c_v_norm[j], c_w_s[j], c_b_s[j], c_w_out[j])
        else:
            y = conformer_conv_mixer(h, d_w_pw1[j], d_conv[j], d_conv_b[j],
                                     d_ln_g[j], d_ln_b[j], d_w_pw2[j])
        x = x + y
        x = x + cross_attention(_rmsnorm(x, xattn_norm[i]), memn,
                                xa_wq[i], xa_wk[i], xa_wv[i], xa_wo[i])
        h = _rmsnorm(x, ffn_norm[i])
        k = i // 2
        if i % 2 == 0:
            x = x + swiglu(h, f_w1[k], f_w3[k], f_w2[k])
        else:
            x = x + moe_ffn(h, m_router[k], m_w1[k], m_w3[k], m_w2[k])
    return _rmsnorm(x, final_norm)
```

```python
import functools

import jax
import jax.numpy as jnp
from jax import lax
from jax.experimental import pallas as pl
from jax.experimental.pallas import tpu as pltpu

NORM_EPS = 1e-6
XATTN_HEADS = 4
POOL_WINDOWS = (2, 4, 8, 16)
GMLP_CHUNK = 128
GMLP_GROUPS = 4
TOP_K = 2

BF16 = jnp.bfloat16
F32 = jnp.float32

V7X_VMEM_BYTES = 64 * 1024 * 1024
V7X_LANES = 128
VMEM_LIMIT = 56 * 1024 * 1024

TM = 512
TN = 512
TM_MOE = 512
TM_ROW = 512
HALO_A = 16
HALO_B = 16
HALO_D = 32


def _cparams(semantics):
    return pltpu.CompilerParams(dimension_semantics=semantics,
                                vmem_limit_bytes=VMEM_LIMIT)


def _rms_scale(x):
    return lax.rsqrt(jnp.mean(x * x, axis=-1, keepdims=True) + NORM_EPS)


def _rmsnorm_f32(x, g):
    return (x * _rms_scale(x)) * g


def _dot(a, b):
    return jnp.dot(a, b, preferred_element_type=F32)


def _shift_rows(z, s):
    if s == 0:
        return z
    return pltpu.roll(z, s, axis=0)


def _halo_keep_mask(rows, halo, first_tile):
    r = lax.broadcasted_iota(jnp.int32, (rows, 1), 0)
    return jnp.logical_or(r >= halo, jnp.logical_not(first_tile))


def _rms_mm_kernel(x_ref, g_ref, w_ref, o_ref, hn_ref):
    @pl.when(pl.program_id(1) == 0)
    def _():
        hn_ref[...] = _rmsnorm_f32(x_ref[...], g_ref[...]).astype(BF16)

    o_ref[...] = _dot(hn_ref[...], w_ref[...]).astype(o_ref.dtype)


def rms_matmul(x, g, w, *, tm, tn):
    m, d = x.shape
    n = w.shape[1]
    return pl.pallas_call(
        _rms_mm_kernel,
        grid=(m // tm, n // tn),
        in_specs=[pl.BlockSpec((tm, d), lambda i, j: (i, 0)),
                  pl.BlockSpec((1, d), lambda i, j: (0, 0)),
                  pl.BlockSpec((d, tn), lambda i, j: (0, j))],
        out_specs=pl.BlockSpec((tm, tn), lambda i, j: (i, j)),
        out_shape=jax.ShapeDtypeStruct((m, n), BF16),
        scratch_shapes=[pltpu.VMEM((tm, d), BF16)],
        compiler_params=_cparams(("parallel", "arbitrary")),
        name="mem_kv_proj",
    )(x, g, w)


def _xattn_kernel(x_ref, g_ref, wq_ref, k_ref, v_ref, wo_ref, o_ref, hn_ref, *, scale):
    @pl.when(pl.program_id(1) == 0)
    def _():
        x = x_ref[...]
        hn_ref[...] = _rmsnorm_f32(x, g_ref[...]).astype(BF16)
        o_ref[...] = x

    q = _dot(hn_ref[...], wq_ref[...]).astype(BF16)
    s = lax.dot_general(q, k_ref[...], (((1,), (1,)), ((), ())),
                        preferred_element_type=F32) * scale
    e = jnp.exp(s - jnp.max(s, axis=-1, keepdims=True))
    p = (e / jnp.sum(e, axis=-1, keepdims=True)).astype(BF16)
    oh = _dot(p, v_ref[...]).astype(BF16)
    o_ref[...] += _dot(oh, wo_ref[...])


def cross_attention(x, g, wq, kv, layer, wo, *, seq, tm):
    t, d = x.shape
    dh = d // XATTN_HEADS
    mem = kv.shape[1]
    tiles_per_seq = seq // tm
    kcol = 2 * layer * XATTN_HEADS
    vcol = kcol + XATTN_HEADS
    return pl.pallas_call(
        functools.partial(_xattn_kernel, scale=float(dh) ** -0.5),
        grid=(t // tm, XATTN_HEADS),
        in_specs=[pl.BlockSpec((tm, d), lambda i, h: (i, 0)),
                  pl.BlockSpec((1, d), lambda i, h: (0, 0)),
                  pl.BlockSpec((d, dh), lambda i, h: (0, h)),
                  pl.BlockSpec((None, mem, dh), lambda i, h: (i // tiles_per_seq, 0, kcol + h)),
                  pl.BlockSpec((None, mem, dh), lambda i, h: (i // tiles_per_seq, 0, vcol + h)),
                  pl.BlockSpec((dh, d), lambda i, h: (h, 0))],
        out_specs=pl.BlockSpec((tm, d), lambda i, h: (i, 0)),
        out_shape=jax.ShapeDtypeStruct((t, d), F32),
        scratch_shapes=[pltpu.VMEM((tm, d), BF16)],
        compiler_params=_cparams(("parallel", "arbitrary")),
        name="cross_attention",
    )(x, g, wq, kv, kv, wo)


def _ffn_kernel(x_ref, g_ref, w1_ref, w3_ref, w2_ref, o_ref, hn_ref):
    @pl.when(pl.program_id(1) == 0)
    def _():
        x = x_ref[...]
        hn_ref[...] = _rmsnorm_f32(x, g_ref[...]).astype(BF16)
        o_ref[...] = x

    hn = hn_ref[...]
    a = _dot(hn, w1_ref[...])
    b = _dot(hn, w3_ref[...])
    act = (a * jax.nn.sigmoid(a) * b).astype(BF16)
    o_ref[...] += _dot(act, w2_ref[...])


def swiglu_ffn(x, g, w1, w3, w2, *, tm, tf):
    t, d = x.shape
    f = w1.shape[1]
    return pl.pallas_call(
        _ffn_kernel,
        grid=(t // tm, f // tf),
        in_specs=[pl.BlockSpec((tm, d), lambda i, j: (i, 0)),
                  pl.BlockSpec((1, d), lambda i, j: (0, 0)),
                  pl.BlockSpec((d, tf), lambda i, j: (0, j)),
                  pl.BlockSpec((d, tf), lambda i, j: (0, j)),
                  pl.BlockSpec((tf, d), lambda i, j: (j, 0))],
        out_specs=pl.BlockSpec((tm, d), lambda i, j: (i, 0)),
        out_shape=jax.ShapeDtypeStruct((t, d), F32),
        scratch_shapes=[pltpu.VMEM((tm, d), BF16)],
        compiler_params=_cparams(("parallel", "arbitrary")),
        name="swiglu_ffn",
    )(x, g, w1, w3, w2)


def _mixer_a_kernel(x_ref, xp_ref, g_ref, wb_ref, wc_ref, wz_ref, cw_ref, wo_ref,
                    o_ref, hn_ref, *, halo, tiles_per_seq):
    i = pl.program_id(0)
    tm = x_ref.shape[0]

    @pl.when(pl.program_id(1) == 0)
    def _():
        x = x_ref[...]
        g = g_ref[...]
        hn_ref[pl.ds(halo, tm), :] = _rmsnorm_f32(x, g).astype(BF16)
        hn_ref[pl.ds(0, halo), :] = _rmsnorm_f32(xp_ref[...], g).astype(BF16)
        o_ref[...] = x

    hfull = hn_ref[...]
    bgate = _dot(hn_ref[pl.ds(halo, tm), :], wb_ref[...])
    u = _dot(hfull, wc_ref[...]) * _dot(hfull, wz_ref[...])
    keep = _halo_keep_mask(halo + tm, halo, i % tiles_per_seq == 0)
    u = jnp.where(keep, u, 0.0)
    kw = cw_ref.shape[0]
    conv = u * cw_ref[kw - 1:kw, :]
    for s in range(1, kw):
        conv = conv + _shift_rows(u, s) * cw_ref[kw - 1 - s:kw - s, :]
    y = (bgate * conv[halo:, :]).astype(BF16)
    o_ref[...] += _dot(y, wo_ref[...])


def mixer_short_conv(x, g, w_in, conv_w, w_out, *, seq, tm, tn):
    t, d = x.shape
    nd = d // tn
    halo = HALO_A
    assert conv_w.shape[0] - 1 <= halo
    hb = tm // halo
    kern = functools.partial(_mixer_a_kernel, halo=halo, tiles_per_seq=seq // tm)
    return pl.pallas_call(
        kern,
        grid=(t // tm, nd),
        in_specs=[pl.BlockSpec((tm, d), lambda i, j: (i, 0)),
                  pl.BlockSpec((halo, d), lambda i, j: (jnp.maximum(i * hb - 1, 0), 0)),
                  pl.BlockSpec((1, d), lambda i, j: (0, 0)),
                  pl.BlockSpec((d, tn), lambda i, j: (0, j)),
                  pl.BlockSpec((d, tn), lambda i, j: (0, nd + j)),
                  pl.BlockSpec((d, tn), lambda i, j: (0, 2 * nd + j)),
                  pl.BlockSpec((conv_w.shape[0], tn), lambda i, j: (0, j)),
                  pl.BlockSpec((tn, d), lambda i, j: (j, 0))],
        out_specs=pl.BlockSpec((tm, d), lambda i, j: (i, 0)),
        out_shape=jax.ShapeDtypeStruct((t, d), F32),
        scratch_shapes=[pltpu.VMEM((halo + tm, d), BF16)],
        compiler_params=_cparams(("parallel", "arbitrary")),
        name="mixer_short_conv",
    )(x, x, g, w_in, w_in, w_in, conv_w, w_out)


def _mixer_b_kernel(x_ref, xp_ref, g_ref, wm_ref, sc_ref, o_ref, *, halo, tiles_per_seq, windows):
    i = pl.program_id(0)
    tm, d = x_ref.shape
    dg = d // len(windows)
    x = x_ref[...]
    g = g_ref[...]
    first = i % tiles_per_seq == 0
    h_cur = _rmsnorm_f32(x, g)
    h_prev = jnp.where(first, 0.0, _rmsnorm_f32(xp_ref[...], g))
    h = jnp.concatenate([h_prev, h_cur], axis=0)
    pos = (i % tiles_per_seq) * tm + lax.broadcasted_iota(jnp.int32, (tm, 1), 0)
    for gi, w in enumerate(windows):
        hg = h[:, gi * dg:(gi + 1) * dg]
        wsum = hg
        span = 1
        while span < w:
            wsum = wsum + _shift_rows(wsum, span)
            span *= 2
        count = jnp.minimum(pos + 1, w).astype(F32)
        pooled = wsum[halo:, :] / count - hg[halo:, :]
        y = _dot(pooled.astype(BF16), wm_ref[gi]) * sc_ref[:, gi * dg:(gi + 1) * dg]
        o_ref[:, gi * dg:(gi + 1) * dg] = x[:, gi * dg:(gi + 1) * dg] + y


def mixer_pooling(x, g, w_map, scale, *, seq, tm):
    t, d = x.shape
    halo = HALO_B
    assert max(POOL_WINDOWS) <= halo and all(w & (w - 1) == 0 for w in POOL_WINDOWS)
    hb = tm // halo
    kern = functools.partial(_mixer_b_kernel, halo=halo, tiles_per_seq=seq // tm,
                             windows=POOL_WINDOWS)
    return pl.pallas_call(
        kern,
        grid=(t // tm,),
        in_specs=[pl.BlockSpec((tm, d), lambda i: (i, 0)),
                  pl.BlockSpec((halo, d), lambda i: (jnp.maximum(i * hb - 1, 0), 0)),
                  pl.BlockSpec((1, d), lambda i: (0, 0)),
                  pl.BlockSpec(w_map.shape, lambda i: (0, 0, 0)),
                  pl.BlockSpec((1, d), lambda i: (0, 0))],
        out_specs=pl.BlockSpec((tm, d), lambda i: (i, 0)),
        out_shape=jax.ShapeDtypeStruct((t, d), F32),
        compiler_params=_cparams(("parallel",)),
        name="mixer_pooling",
    )(x, x, g, w_map, scale)


def _gelu(x):
    return 0.5 * x * (1.0 + lax.erf(x * (2.0 ** -0.5)))


def _mixer_c_kernel(x_ref, g_ref, wuv_ref, vg_ref, ws_ref, bs_ref, wo_ref,
                    o_ref, hn_ref, v_ref, ssq_ref, *, nd, chunk):
    j = pl.program_id(1)
    tm = x_ref.shape[0]

    @pl.when(j == 0)
    def _():
        x = x_ref[...]
        hn_ref[...] = _rmsnorm_f32(x, g_ref[...]).astype(BF16)
        o_ref[...] = x
        ssq_ref[...] = jnp.zeros_like(ssq_ref)

    act = _gelu(_dot(hn_ref[...], wuv_ref[...]))

    @pl.when(j < nd)
    def _():
        v_ref[j] = act
        ssq_ref[...] += jnp.sum(act * act, axis=-1, keepdims=True)

    @pl.when(j >= nd)
    def _():
        grp = j - nd
        width = v_ref.shape[0] * v_ref.shape[2]
        r = lax.rsqrt(ssq_ref[...] * (1.0 / width) + NORM_EPS)
        vn = ((v_ref[grp] * r) * vg_ref[0]).astype(BF16)
        row = lax.broadcasted_iota(jnp.int32, (chunk, chunk), 0)
        col = lax.broadcasted_iota(jnp.int32, (chunk, chunk), 1)
        ws = jnp.where(row >= col, ws_ref[0], 0.0).astype(BF16)
        bias = bs_ref[0]
        parts = []
        for c in range(tm // chunk):
            parts.append(_dot(ws, vn[c * chunk:(c + 1) * chunk, :]) + bias)
        sv = jnp.concatenate(parts, axis=0)
        o_ref[...] += _dot((act * sv).astype(BF16), wo_ref[...])


def mixer_gmlp(x, g, w_uv, v_gain, w_s, b_s, w_out, *, tm):
    t, d = x.shape
    width = w_out.shape[0]
    tn = width // GMLP_GROUPS
    nd = GMLP_GROUPS
    chunk = GMLP_CHUNK
    assert tm % chunk == 0
    vg = v_gain.reshape(nd, 1, tn)
    bs = b_s.reshape(nd, chunk, 1)
    kern = functools.partial(_mixer_c_kernel, nd=nd, chunk=chunk)
    return pl.pallas_call(
        kern,
        grid=(t // tm, 2 * nd),
        in_specs=[pl.BlockSpec((tm, d), lambda i, j: (i, 0)),
                  pl.BlockSpec((1, d), lambda i, j: (0, 0)),
                  pl.BlockSpec((d, tn), lambda i, j: (0, jnp.where(j < nd, j + nd, j - nd))),
                  pl.BlockSpec((1, 1, tn), lambda i, j: (jnp.maximum(j - nd, 0), 0, 0)),
                  pl.BlockSpec((1, chunk, chunk), lambda i, j: (jnp.maximum(j - nd, 0), 0, 0)),
                  pl.BlockSpec((1, chunk, 1), lambda i, j: (jnp.maximum(j - nd, 0), 0, 0)),
                  pl.BlockSpec((tn, d), lambda i, j: (jnp.maximum(j - nd, 0), 0))],
        out_specs=pl.BlockSpec((tm, d), lambda i, j: (i, 0)),
        out_shape=jax.ShapeDtypeStruct((t, d), F32),
        scratch_shapes=[pltpu.VMEM((tm, d), BF16),
                        pltpu.VMEM((nd, tm, tn), F32),
                        pltpu.VMEM((tm, 1), F32)],
        compiler_params=_cparams(("parallel", "arbitrary")),
        name="mixer_gmlp",
    )(x, g, w_uv, vg, w_s, bs, w_out)


def _mixer_d_kernel(x_ref, xp_ref, g_ref, wa_ref, wg_ref, cw_ref, cb_ref, lg_ref, lb_ref,
                    wo_ref, o_ref, hn_ref, z_ref, mean_ref, rstd_ref,
                    *, nd, halo, tiles_per_seq):
    i = pl.program_id(0)
    j = pl.program_id(1)
    tm = x_ref.shape[0]
    width = z_ref.shape[0] * z_ref.shape[2]

    @pl.when(j == 0)
    def _():
        x = x_ref[...]
        g = g_ref[...]
        hn_ref[pl.ds(halo, tm), :] = _rmsnorm_f32(x, g).astype(BF16)
        hn_ref[pl.ds(0, halo), :] = _rmsnorm_f32(xp_ref[...], g).astype(BF16)
        o_ref[...] = x
        mean_ref[...] = jnp.zeros_like(mean_ref)

    @pl.when(j < nd)
    def _():
        hfull = hn_ref[...]
        a = _dot(hfull, wa_ref[...])
        gate = _dot(hfull, wg_ref[...])
        z = a * jax.nn.sigmoid(gate)
        keep = _halo_keep_mask(halo + tm, halo, i % tiles_per_seq == 0)
        z = jnp.where(keep, z, 0.0)
        kw = cw_ref.shape[0]
        acc = None
        for b in range(8):
            zb = _shift_rows(z, b)
            for s in range(b, kw, 8):
                term = zb[halo - (s - b):halo - (s - b) + tm, :] * cw_ref[kw - 1 - s:kw - s, :]
                acc = term if acc is None else acc + term
        acc = acc + cb_ref[0]
        z_ref[j] = acc
        mean_ref[...] += jnp.sum(acc, axis=-1, keepdims=True)

    @pl.when(j == nd)
    def _():
        mean = mean_ref[...] * (1.0 / width)
        mean_ref[...] = mean
        var = jnp.zeros_like(mean)
        for k in range(nd):
            c = z_ref[k] - mean
            var = var + jnp.sum(c * c, axis=-1, keepdims=True)
        rstd_ref[...] = lax.rsqrt(var * (1.0 / width) + NORM_EPS)

    @pl.when(j >= nd)
    def _():
        zn = (z_ref[j - nd] - mean_ref[...]) * rstd_ref[...]
        zn = zn * lg_ref[0] + lb_ref[0]
        act = (zn * jax.nn.sigmoid(zn)).astype(BF16)
        o_ref[...] += _dot(act, wo_ref[...])


def mixer_conformer(x, g, w_pw1, conv_w, conv_b, ln_g, ln_b, w_pw2, *, seq, tm, tn):
    t, d = x.shape
    nd = d // tn
    halo = HALO_D
    kw = conv_w.shape[0]
    assert kw - 1 <= halo and halo % 16 == 0
    hb = tm // halo
    lo = lambda j: jnp.minimum(j, nd - 1)
    hi = lambda j: jnp.maximum(j - nd, 0)
    kern = functools.partial(_mixer_d_kernel, nd=nd, halo=halo, tiles_per_seq=seq // tm)
    vec = lambda a: a.reshape(nd, 1, tn)
    return pl.pallas_call(
        kern,
        grid=(t // tm, 2 * nd),
        in_specs=[pl.BlockSpec((tm, d), lambda i, j: (i, 0)),
                  pl.BlockSpec((halo, d), lambda i, j: (jnp.maximum(i * hb - 1, 0), 0)),
                  pl.BlockSpec((1, d), lambda i, j: (0, 0)),
                  pl.BlockSpec((d, tn), lambda i, j: (0, lo(j))),
                  pl.BlockSpec((d, tn), lambda i, j: (0, nd + lo(j))),
                  pl.BlockSpec((kw, tn), lambda i, j: (0, lo(j))),
                  pl.BlockSpec((1, 1, tn), lambda i, j: (lo(j), 0, 0)),
                  pl.BlockSpec((1, 1, tn), lambda i, j: (hi(j), 0, 0)),
                  pl.BlockSpec((1, 1, tn), lambda i, j: (hi(j), 0, 0)),
                  pl.BlockSpec((tn, d), lambda i, j: (hi(j), 0))],
        out_specs=pl.BlockSpec((tm, d), lambda i, j: (i, 0)),
        out_shape=jax.ShapeDtypeStruct((t, d), F32),
        scratch_shapes=[pltpu.VMEM((halo + tm, d), BF16),
                        pltpu.VMEM((nd, tm, tn), F32),
                        pltpu.VMEM((tm, 1), F32),
                        pltpu.VMEM((tm, 1), F32)],
        compiler_params=_cparams(("parallel", "arbitrary")),
        name="mixer_conformer",
    )(x, x, g, w_pw1, w_pw1, conv_w, vec(conv_b), vec(ln_g), vec(ln_b), w_pw2)


def _router_kernel(x_ref, g_ref, rt_ref, e1_ref, e2_ref, g1_ref, g2_ref, r1_ref, r2_ref,
                   cnt_ref, carry_ref, *, n_experts):
    tm = x_ref.shape[0]

    @pl.when(pl.program_id(0) == 0)
    def _():
        carry_ref[...] = jnp.zeros_like(carry_ref)

    h = _rmsnorm_f32(x_ref[...], g_ref[...])
    logits = [jnp.sum(h * rt_ref[e:e + 1, :], axis=-1, keepdims=True) for e in range(n_experts)]
    m1 = logits[0]
    i1 = jnp.zeros((tm, 1), jnp.int32)
    for e in range(1, n_experts):
        better = logits[e] > m1
        m1 = jnp.where(better, logits[e], m1)
        i1 = jnp.where(better, e, i1)
    m2 = jnp.full((tm, 1), -jnp.inf, F32)
    i2 = jnp.where(i1 == 0, 1, 0).astype(jnp.int32)
    for e in range(n_experts):
        better = jnp.logical_and(i1 != e, logits[e] > m2)
        m2 = jnp.where(better, logits[e], m2)
        i2 = jnp.where(better, e, i2)
    ex = jnp.exp(m2 - m1)
    denom = 1.0 + ex
    e1_ref[...] = i1
    e2_ref[...] = i2
    g1_ref[...] = 1.0 / denom
    g2_ref[...] = ex / denom

    lane = lax.broadcasted_iota(jnp.int32, (tm, V7X_LANES), 1)
    hit1 = lane == i1
    hit2 = lane == i2
    sel = jnp.logical_or(hit1, hit2).astype(BF16)
    row = lax.broadcasted_iota(jnp.int32, (tm, tm), 0)
    col = lax.broadcasted_iota(jnp.int32, (tm, tm), 1)
    before = (col < row).astype(BF16)
    ranks = _dot(before, sel) + carry_ref[...]
    r1_ref[...] = jnp.sum(jnp.where(hit1, ranks, 0.0), axis=-1, keepdims=True).astype(jnp.int32)
    r2_ref[...] = jnp.sum(jnp.where(hit2, ranks, 0.0), axis=-1, keepdims=True).astype(jnp.int32)
    total = carry_ref[...] + jnp.sum(sel.astype(F32), axis=0, keepdims=True)
    carry_ref[...] = total
    cnt_ref[...] = jnp.broadcast_to(total, cnt_ref.shape).astype(jnp.int32)


def moe_router(x, g, router_t, *, tm):
    t, d = x.shape
    n_experts = router_t.shape[0]
    col = lambda dt: jax.ShapeDtypeStruct((t, 1), dt)
    cspec = pl.BlockSpec((tm, 1), lambda i: (i, 0))
    return pl.pallas_call(
        functools.partial(_router_kernel, n_experts=n_experts),
        grid=(t // tm,),
        in_specs=[pl.BlockSpec((tm, d), lambda i: (i, 0)),
                  pl.BlockSpec((1, d), lambda i: (0, 0)),
                  pl.BlockSpec((n_experts, d), lambda i: (0, 0))],
        out_specs=[cspec, cspec, cspec, cspec, cspec, cspec,
                   pl.BlockSpec((8, V7X_LANES), lambda i: (0, 0))],
        out_shape=[col(jnp.int32), col(jnp.int32), col(F32), col(F32),
                   col(jnp.int32), col(jnp.int32),
                   jax.ShapeDtypeStruct((8, V7X_LANES), jnp.int32)],
        scratch_shapes=[pltpu.VMEM((1, V7X_LANES), F32)],
        compiler_params=_cparams(("arbitrary",)),
        name="moe_router",
    )(x, g, router_t)


def _row_copy_wait(src_ref, dst_ref, sem, n_rows_src):
    pltpu.make_async_copy(src_ref, dst_ref.at[pl.ds(0, n_rows_src)], sem).wait()


def _dispatch_kernel(p1_ref, p2_ref, x_ref, g_ref, xs_in_ref, xs_ref, h_ref, sem):
    del xs_in_ref
    i = pl.program_id(0)
    tm = x_ref.shape[0]
    h_ref[...] = _rmsnorm_f32(x_ref[...], g_ref[...])

    def issue(r, carry):
        t = i * tm + r
        src = h_ref.at[pl.ds(r, 1)]
        pltpu.make_async_copy(src, xs_ref.at[pl.ds(p1_ref[t], 1)], sem).start()
        pltpu.make_async_copy(src, xs_ref.at[pl.ds(p2_ref[t], 1)], sem).start()
        return carry

    lax.fori_loop(0, tm, issue, 0)
    for _ in range(TOP_K):
        _row_copy_wait(h_ref, xs_ref, sem, tm)


def moe_dispatch(x, g, pos1, pos2, n_rows, *, tm):
    t, d = x.shape
    xs0 = jnp.zeros((n_rows, d), F32)
    grid_spec = pltpu.PrefetchScalarGridSpec(
        num_scalar_prefetch=2,
        grid=(t // tm,),
        in_specs=[pl.BlockSpec((tm, d), lambda i, p1, p2: (i, 0)),
                  pl.BlockSpec((1, d), lambda i, p1, p2: (0, 0)),
                  pl.BlockSpec(memory_space=pl.ANY)],
        out_specs=pl.BlockSpec(memory_space=pl.ANY),
        scratch_shapes=[pltpu.VMEM((tm, d), F32), pltpu.SemaphoreType.DMA(())],
    )
    return pl.pallas_call(
        _dispatch_kernel,
        grid_spec=grid_spec,
        out_shape=jax.ShapeDtypeStruct((n_rows, d), F32),
        input_output_aliases={4: 0},
        compiler_params=_cparams(("arbitrary",)),
        name="moe_dispatch",
    )(pos1, pos2, x, g, xs0)


def _moe_ffn_kernel(te_ref, xs_ref, w1_ref, w3_ref, w2_ref, ys_ref, hn_ref, *, n_tiles):
    i = pl.program_id(0)
    active = i < te_ref[n_tiles]

    @pl.when(pl.program_id(1) == 0)
    def _():
        hn_ref[...] = xs_ref[...].astype(BF16)
        ys_ref[...] = jnp.zeros_like(ys_ref)

    @pl.when(active)
    def _():
        hn = hn_ref[...]
        a = _dot(hn, w1_ref[...])
        b = _dot(hn, w3_ref[...])
        act = (a * jax.nn.sigmoid(a) * b).astype(BF16)
        ys_ref[...] += _dot(act, w2_ref[...])


def moe_grouped_ffn(xs, tile_expert, w1, w3, w2, *, tm, tf):
    n_rows, d = xs.shape
    f = w1.shape[2]
    n_tiles = n_rows // tm
    nf = f // tf

    def jj(i, j, te):
        return jnp.where(i < te[n_tiles], j, nf - 1)

    grid_spec = pltpu.PrefetchScalarGridSpec(
        num_scalar_prefetch=1,
        grid=(n_tiles, nf),
        in_specs=[pl.BlockSpec((tm, d), lambda i, j, te: (jnp.minimum(i, te[n_tiles] - 1), 0)),
                  pl.BlockSpec((None, d, tf), lambda i, j, te: (te[i], 0, jj(i, j, te))),
                  pl.BlockSpec((None, d, tf), lambda i, j, te: (te[i], 0, jj(i, j, te))),
                  pl.BlockSpec((None, tf, d), lambda i, j, te: (te[i], jj(i, j, te), 0))],
        out_specs=pl.BlockSpec((tm, d), lambda i, j, te: (i, 0)),
        scratch_shapes=[pltpu.VMEM((tm, d), BF16)],
    )
    return pl.pallas_call(
        functools.partial(_moe_ffn_kernel, n_tiles=n_tiles),
        grid_spec=grid_spec,
        out_shape=jax.ShapeDtypeStruct((n_rows, d), F32),
        compiler_params=_cparams(("arbitrary", "arbitrary")),
        name="moe_grouped_ffn",
    )(tile_expert, xs, w1, w3, w2)


def _combine_kernel(p1_ref, p2_ref, x_ref, g1_ref, g2_ref, ys_ref, *rest, final_norm):
    if final_norm:
        fg_ref, o_ref, a_ref, b_ref, sem = rest
    else:
        o_ref, a_ref, b_ref, sem = rest
    i = pl.program_id(0)
    tm = x_ref.shape[0]

    def issue(r, carry):
        t = i * tm + r
        pltpu.make_async_copy(ys_ref.at[pl.ds(p1_ref[t], 1)], a_ref.at[pl.ds(r, 1)], sem).start()
        pltpu.make_async_copy(ys_ref.at[pl.ds(p2_ref[t], 1)], b_ref.at[pl.ds(r, 1)], sem).start()
        return carry

    lax.fori_loop(0, tm, issue, 0)
    _row_copy_wait(a_ref, ys_ref, sem, tm)
    _row_copy_wait(b_ref, ys_ref, sem, tm)
    y = x_ref[...] + (g1_ref[...] * a_ref[...] + g2_ref[...] * b_ref[...])
    if final_norm:
        y = _rmsnorm_f32(y, fg_ref[...])
    o_ref[...] = y


def moe_combine(x, g1, g2, ys, pos1, pos2, final_g, *, tm):
    t, d = x.shape
    final_norm = final_g is not None
    in_specs = [pl.BlockSpec((tm, d), lambda i, p1, p2: (i, 0)),
                pl.BlockSpec((tm, 1), lambda i, p1, p2: (i, 0)),
                pl.BlockSpec((tm, 1), lambda i, p1, p2: (i, 0)),
                pl.BlockSpec(memory_space=pl.ANY)]
    args = [pos1, pos2, x, g1, g2, ys]
    if final_norm:
        in_specs.append(pl.BlockSpec((1, d), lambda i, p1, p2: (0, 0)))
        args.append(final_g)
    grid_spec = pltpu.PrefetchScalarGridSpec(
        num_scalar_prefetch=2,
        grid=(t // tm,),
        in_specs=in_specs,
        out_specs=pl.BlockSpec((tm, d), lambda i, p1, p2: (i, 0)),
        scratch_shapes=[pltpu.VMEM((tm, d), F32), pltpu.VMEM((tm, d), F32),
                        pltpu.SemaphoreType.DMA(())],
    )
    return pl.pallas_call(
        functools.partial(_combine_kernel, final_norm=final_norm),
        grid_spec=grid_spec,
        out_shape=jax.ShapeDtypeStruct((t, d), F32),
        compiler_params=_cparams(("arbitrary",)),
        name="moe_combine",
    )(*args)


def moe_ffn(x, g, router, w1, w3, w2, final_g, *, tm_moe, tf):
    t, d = x.shape
    n_experts = router.shape[1]
    e1, e2, g1, g2, r1, r2, cnt = moe_router(x, g, router.T, tm=TM)

    counts = cnt[0, :n_experts]
    padded = ((counts + tm_moe - 1) // tm_moe) * tm_moe
    ends = jnp.cumsum(padded)
    starts = ends - padded
    pos1 = (starts[e1[:, 0]] + r1[:, 0]).astype(jnp.int32)
    pos2 = (starts[e2[:, 0]] + r2[:, 0]).astype(jnp.int32)
    n_rows = TOP_K * t + n_experts * tm_moe
    n_tiles = n_rows // tm_moe
    n_active = (ends[-1] // tm_moe).astype(jnp.int32)
    tile_row = jnp.arange(n_tiles, dtype=jnp.int32) * tm_moe
    tile_e = jnp.sum((tile_row[:, None] >= ends[None, :]).astype(jnp.int32), axis=1)
    tile_e = jnp.minimum(tile_e, n_experts - 1)
    last_e = tile_e[jnp.maximum(n_active - 1, 0)]
    tile_e = jnp.where(jnp.arange(n_tiles) < n_active, tile_e, last_e)
    tile_expert = jnp.concatenate([tile_e, n_active[None]]).astype(jnp.int32)

    xs = moe_dispatch(x, g, pos1, pos2, n_rows, tm=TM_ROW)
    ys = moe_grouped_ffn(xs, tile_expert, w1, w3, w2, tm=tm_moe, tf=tf)
    return moe_combine(x, g1, g2, ys, pos1, pos2, final_g, tm=TM_ROW)


def kernel(x, mem, mem_norm, final_norm, mixer_norm, xattn_norm, ffn_norm, xa_wq, xa_wk, xa_wv, xa_wo, a_w_in, a_conv, a_w_out, b_w_map, b_scale, c_w_uv, c_v_norm, c_w_s, c_b_s, c_w_out, d_w_pw1, d_conv, d_conv_b, d_ln_g, d_ln_b, d_w_pw2, f_w1, f_w3, f_w2, m_router, m_w1, m_w3, m_w2):
    batch, seq, d = x.shape
    depth = mixer_norm.shape[0]
    n_mixers = 4
    bf = lambda w: w.astype(BF16)
    row = lambda v: v.reshape(1, -1)

    mem2 = mem.reshape(batch * mem.shape[1], d)
    wkv = jnp.concatenate([bf(w) for l in range(depth) for w in (xa_wk[l], xa_wv[l])], axis=1)
    kv = rms_matmul(mem2, row(mem_norm), wkv, tm=mem2.shape[0], tn=TN)
    kv = kv.reshape(batch, mem.shape[1], 2 * depth * d)

    xt = x.reshape(batch * seq, d)
    for i in range(depth):
        mixer, j = i % n_mixers, i // n_mixers
        g = row(mixer_norm[i])
        if mixer == 0:
            xt = mixer_short_conv(xt, g, bf(a_w_in[j]), a_conv[j], bf(a_w_out[j]),
                                  seq=seq, tm=TM, tn=TN)
        elif mixer == 1:
            xt = mixer_pooling(xt, g, bf(b_w_map[j]), row(b_scale[j]), seq=seq, tm=TM)
        elif mixer == 2:
            xt = mixer_gmlp(xt, g, bf(c_w_uv[j]), c_v_norm[j], c_w_s[j], c_b_s[j],
                            bf(c_w_out[j]), tm=TM)
        else:
            xt = mixer_conformer(xt, g, bf(d_w_pw1[j]), d_conv[j], d_conv_b[j], d_ln_g[j],
                                 d_ln_b[j], bf(d_w_pw2[j]), seq=seq, tm=TM, tn=TN)
        xt = cross_attention(xt, row(xattn_norm[i]), bf(xa_wq[i]), kv, i, bf(xa_wo[i]),
                             seq=seq, tm=TM)
        k = i // 2
        gf = row(ffn_norm[i])
        if i % 2 == 0:
            xt = swiglu_ffn(xt, gf, bf(f_w1[k]), bf(f_w3[k]), bf(f_w2[k]), tm=TM, tf=TN)
        else:
            fin = row(final_norm) if i == depth - 1 else None
            xt = moe_ffn(xt, gf, m_router[k], bf(m_w1[k]), bf(m_w3[k]), bf(m_w2[k]), fin,
                         tm_moe=TM_MOE, tf=TN)
    if depth % 2 == 1:
        raise NotImplementedError("final RMSNorm is fused into the last (expert) layer")
    return xt.reshape(batch, seq, d)
```

```python
import functools

import jax
import jax.numpy as jnp
from jax import lax
from jax.experimental import pallas as pl
from jax.experimental.pallas import tpu as pltpu

NORM_EPS = 1e-6
XATTN_HEADS = 4
POOL_WINDOWS = (2, 4, 8, 16)
GMLP_CHUNK = 128
GMLP_GROUPS = 4
TOP_K = 2

BF16 = jnp.bfloat16
F32 = jnp.float32

V7X_VMEM_BYTES = 64 * 1024 * 1024
V7X_LANES = 128
VMEM_LIMIT = 56 * 1024 * 1024

TM = 1024
TM_VPU = 512
TN = 512
TM_MOE = 1024
TM_ROW = 512
HALO_A = 16
HALO_B = 16
HALO_D = 32


def _cparams(semantics):
    return pltpu.CompilerParams(dimension_semantics=semantics,
                                vmem_limit_bytes=VMEM_LIMIT)


def _rms_scale(x):
    return lax.rsqrt(jnp.mean(x * x, axis=-1, keepdims=True) + NORM_EPS)


def _rmsnorm_f32(x, g):
    return (x * _rms_scale(x)) * g


def _dot(a, b):
    return jnp.dot(a, b, preferred_element_type=F32)


def _shift_rows(z, s):
    if s == 0:
        return z
    return pltpu.roll(z, s, axis=0)


def _tile_copy(src_hbm, buf, sem, tile):
    tm = buf.shape[0]
    start = pl.multiple_of(tile * tm, tm)
    return pltpu.make_async_copy(src_hbm.at[pl.ds(start, tm)], buf, sem)


def _await_tile(src_hbm, buf, sem, i):
    @pl.when(i == 0)
    def _():
        _tile_copy(src_hbm, buf, sem, 0).start()

    _tile_copy(src_hbm, buf, sem, i).wait()


def _request_tile(src_hbm, buf, sem, nxt, n_tiles):
    @pl.when(nxt < n_tiles)
    def _():
        _tile_copy(src_hbm, buf, sem, nxt).start()


_XBUF = lambda tm, d: [pltpu.VMEM((tm, d), F32), pltpu.SemaphoreType.DMA(())]
_ANY = pl.BlockSpec(memory_space=pl.ANY)
_SEQ2 = ("arbitrary", "arbitrary")


def _halo_keep_mask(rows, halo, first_tile):
    r = lax.broadcasted_iota(jnp.int32, (rows, 1), 0)
    return jnp.logical_or(r >= halo, jnp.logical_not(first_tile))


def _rms_mm_kernel(x_ref, g_ref, w_ref, o_ref, hn_ref):
    @pl.when(pl.program_id(1) == 0)
    def _():
        hn_ref[...] = _rmsnorm_f32(x_ref[...], g_ref[...]).astype(BF16)

    o_ref[...] = _dot(hn_ref[...], w_ref[...]).astype(o_ref.dtype)


def rms_matmul(x, g, w, *, tm, tn):
    m, d = x.shape
    n = w.shape[1]
    return pl.pallas_call(
        _rms_mm_kernel,
        grid=(m // tm, n // tn),
        in_specs=[pl.BlockSpec((tm, d), lambda i, j: (i, 0)),
                  pl.BlockSpec((1, d), lambda i, j: (0, 0)),
                  pl.BlockSpec((d, tn), lambda i, j: (0, j))],
        out_specs=pl.BlockSpec((tm, tn), lambda i, j: (i, j)),
        out_shape=jax.ShapeDtypeStruct((m, n), BF16),
        scratch_shapes=[pltpu.VMEM((tm, d), BF16)],
        compiler_params=_cparams(("parallel", "arbitrary")),
        name="mem_kv_proj",
    )(x, g, w)


def _xattn_kernel(x_hbm, g_ref, wq_ref, k_ref, v_ref, wo_ref, o_ref, hn_ref, xbuf, xsem,
                  *, scale):
    i = pl.program_id(0)

    @pl.when(pl.program_id(1) == 0)
    def _():
        _await_tile(x_hbm, xbuf, xsem, i)
        x = xbuf[...]
        hn_ref[...] = _rmsnorm_f32(x, g_ref[...]).astype(BF16)
        o_ref[...] = x
        _request_tile(x_hbm, xbuf, xsem, i + 1, pl.num_programs(0))

    q = _dot(hn_ref[...], wq_ref[...]).astype(BF16)
    s = lax.dot_general(q, k_ref[...], (((1,), (1,)), ((), ())),
                        preferred_element_type=F32) * scale
    e = jnp.exp(s - jnp.max(s, axis=-1, keepdims=True))
    p = (e / jnp.sum(e, axis=-1, keepdims=True)).astype(BF16)
    oh = _dot(p, v_ref[...]).astype(BF16)
    o_ref[...] += _dot(oh, wo_ref[...])


def cross_attention(x, g, wq, kv, layer, wo, *, seq, tm):
    t, d = x.shape
    dh = d // XATTN_HEADS
    mem = kv.shape[1]
    tiles_per_seq = seq // tm
    kcol = 2 * layer * XATTN_HEADS
    vcol = kcol + XATTN_HEADS
    return pl.pallas_call(
        functools.partial(_xattn_kernel, scale=float(dh) ** -0.5),
        grid=(t // tm, XATTN_HEADS),
        in_specs=[_ANY,
                  pl.BlockSpec((1, d), lambda i, h: (0, 0)),
                  pl.BlockSpec((d, dh), lambda i, h: (0, h)),
                  pl.BlockSpec((None, mem, dh), lambda i, h: (i // tiles_per_seq, 0, kcol + h)),
                  pl.BlockSpec((None, mem, dh), lambda i, h: (i // tiles_per_seq, 0, vcol + h)),
                  pl.BlockSpec((dh, d), lambda i, h: (h, 0))],
        out_specs=pl.BlockSpec((tm, d), lambda i, h: (i, 0)),
        out_shape=jax.ShapeDtypeStruct((t, d), F32),
        scratch_shapes=[pltpu.VMEM((tm, d), BF16)] + _XBUF(tm, d),
        compiler_params=_cparams(_SEQ2),
        name="cross_attention",
    )(x, g, wq, kv, kv, wo)


def _ffn_kernel(x_hbm, g_ref, w1_ref, w3_ref, w2_ref, o_ref, hn_ref, xbuf, xsem):
    i = pl.program_id(0)

    @pl.when(pl.program_id(1) == 0)
    def _():
        _await_tile(x_hbm, xbuf, xsem, i)
        x = xbuf[...]
        hn_ref[...] = _rmsnorm_f32(x, g_ref[...]).astype(BF16)
        o_ref[...] = x
        _request_tile(x_hbm, xbuf, xsem, i + 1, pl.num_programs(0))

    hn = hn_ref[...]
    a = _dot(hn, w1_ref[...])
    b = _dot(hn, w3_ref[...])
    act = (a * jax.nn.sigmoid(a) * b).astype(BF16)
    o_ref[...] += _dot(act, w2_ref[...])


def swiglu_ffn(x, g, w1, w3, w2, *, tm, tf):
    t, d = x.shape
    f = w1.shape[1]
    return pl.pallas_call(
        _ffn_kernel,
        grid=(t // tm, f // tf),
        in_specs=[_ANY,
                  pl.BlockSpec((1, d), lambda i, j: (0, 0)),
                  pl.BlockSpec((d, tf), lambda i, j: (0, j)),
                  pl.BlockSpec((d, tf), lambda i, j: (0, j)),
                  pl.BlockSpec((tf, d), lambda i, j: (j, 0))],
        out_specs=pl.BlockSpec((tm, d), lambda i, j: (i, 0)),
        out_shape=jax.ShapeDtypeStruct((t, d), F32),
        scratch_shapes=[pltpu.VMEM((tm, d), BF16)] + _XBUF(tm, d),
        compiler_params=_cparams(_SEQ2),
        name="swiglu_ffn",
    )(x, g, w1, w3, w2)


def _mixer_a_kernel(x_hbm, xp_ref, g_ref, wb_ref, wc_ref, wz_ref, cw_ref, wo_ref,
                    o_ref, hn_ref, xbuf, xsem, *, halo, tiles_per_seq):
    i = pl.program_id(0)
    tm = o_ref.shape[0]

    @pl.when(pl.program_id(1) == 0)
    def _():
        _await_tile(x_hbm, xbuf, xsem, i)
        x = xbuf[...]
        g = g_ref[...]
        hn_ref[pl.ds(halo, tm), :] = _rmsnorm_f32(x, g).astype(BF16)
        hn_ref[pl.ds(0, halo), :] = _rmsnorm_f32(xp_ref[...], g).astype(BF16)
        o_ref[...] = x
        _request_tile(x_hbm, xbuf, xsem, i + 1, pl.num_programs(0))

    hfull = hn_ref[...]
    bgate = _dot(hn_ref[pl.ds(halo, tm), :], wb_ref[...])
    u = _dot(hfull, wc_ref[...]) * _dot(hfull, wz_ref[...])
    keep = _halo_keep_mask(halo + tm, halo, i % tiles_per_seq == 0)
    u = jnp.where(keep, u, 0.0)
    kw = cw_ref.shape[0]
    conv = u * cw_ref[kw - 1:kw, :]
    for s in range(1, kw):
        conv = conv + _shift_rows(u, s) * cw_ref[kw - 1 - s:kw - s, :]
    y = (bgate * conv[halo:, :]).astype(BF16)
    o_ref[...] += _dot(y, wo_ref[...])


def mixer_short_conv(x, g, w_in, conv_w, w_out, *, seq, tm, tn):
    t, d = x.shape
    nd = d // tn
    halo = HALO_A
    assert conv_w.shape[0] - 1 <= halo
    hb = tm // halo
    kern = functools.partial(_mixer_a_kernel, halo=halo, tiles_per_seq=seq // tm)
    return pl.pallas_call(
        kern,
        grid=(t // tm, nd),
        in_specs=[_ANY,
                  pl.BlockSpec((halo, d), lambda i, j: (jnp.maximum(i * hb - 1, 0), 0)),
                  pl.BlockSpec((1, d), lambda i, j: (0, 0)),
                  pl.BlockSpec((d, tn), lambda i, j: (0, j)),
                  pl.BlockSpec((d, tn), lambda i, j: (0, nd + j)),
                  pl.BlockSpec((d, tn), lambda i, j: (0, 2 * nd + j)),
                  pl.BlockSpec((conv_w.shape[0], tn), lambda i, j: (0, j)),
                  pl.BlockSpec((tn, d), lambda i, j: (j, 0))],
        out_specs=pl.BlockSpec((tm, d), lambda i, j: (i, 0)),
        out_shape=jax.ShapeDtypeStruct((t, d), F32),
        scratch_shapes=[pltpu.VMEM((halo + tm, d), BF16)] + _XBUF(tm, d),
        compiler_params=_cparams(_SEQ2),
        name="mixer_short_conv",
    )(x, x, g, w_in, w_in, w_in, conv_w, w_out)


def _mixer_b_kernel(x_ref, xp_ref, g_ref, wm_ref, sc_ref, o_ref, *, halo, tiles_per_seq, windows):
    i = pl.program_id(0)
    tm, d = x_ref.shape
    dg = d // len(windows)
    x = x_ref[...]
    g = g_ref[...]
    first = i % tiles_per_seq == 0
    h_cur = _rmsnorm_f32(x, g)
    h_prev = jnp.where(first, 0.0, _rmsnorm_f32(xp_ref[...], g))
    h = jnp.concatenate([h_prev, h_cur], axis=0)
    pos = (i % tiles_per_seq) * tm + lax.broadcasted_iota(jnp.int32, (tm, 1), 0)
    for gi, w in enumerate(windows):
        hg = h[:, gi * dg:(gi + 1) * dg]
        wsum = hg
        span = 1
        while span < w:
            wsum = wsum + _shift_rows(wsum, span)
            span *= 2
        count = jnp.minimum(pos + 1, w).astype(F32)
        pooled = wsum[halo:, :] / count - hg[halo:, :]
        y = _dot(pooled.astype(BF16), wm_ref[gi]) * sc_ref[:, gi * dg:(gi + 1) * dg]
        o_ref[:, gi * dg:(gi + 1) * dg] = x[:, gi * dg:(gi + 1) * dg] + y


def mixer_pooling(x, g, w_map, scale, *, seq, tm):
    t, d = x.shape
    halo = HALO_B
    assert max(POOL_WINDOWS) <= halo and all(w & (w - 1) == 0 for w in POOL_WINDOWS)
    hb = tm // halo
    kern = functools.partial(_mixer_b_kernel, halo=halo, tiles_per_seq=seq // tm,
                             windows=POOL_WINDOWS)
    return pl.pallas_call(
        kern,
        grid=(t // tm,),
        in_specs=[pl.BlockSpec((tm, d), lambda i: (i, 0)),
                  pl.BlockSpec((halo, d), lambda i: (jnp.maximum(i * hb - 1, 0), 0)),
                  pl.BlockSpec((1, d), lambda i: (0, 0)),
                  pl.BlockSpec(w_map.shape, lambda i: (0, 0, 0)),
                  pl.BlockSpec((1, d), lambda i: (0, 0))],
        out_specs=pl.BlockSpec((tm, d), lambda i: (i, 0)),
        out_shape=jax.ShapeDtypeStruct((t, d), F32),
        compiler_params=_cparams(("parallel",)),
        name="mixer_pooling",
    )(x, x, g, w_map, scale)


def _gelu(x):
    return 0.5 * x * (1.0 + lax.erf(x * (2.0 ** -0.5)))


def _mixer_c_kernel(x_hbm, g_ref, wuv_ref, vg_ref, ws_ref, bs_ref, wo_ref,
                    o_ref, hn_ref, v_ref, ssq_ref, xbuf, xsem, *, nd, chunk):
    i = pl.program_id(0)
    j = pl.program_id(1)
    tm = o_ref.shape[0]

    @pl.when(j == 0)
    def _():
        _await_tile(x_hbm, xbuf, xsem, i)
        x = xbuf[...]
        hn_ref[...] = _rmsnorm_f32(x, g_ref[...]).astype(BF16)
        o_ref[...] = x
        ssq_ref[...] = jnp.zeros_like(ssq_ref)
        _request_tile(x_hbm, xbuf, xsem, i + 1, pl.num_programs(0))

    act = _gelu(_dot(hn_ref[...], wuv_ref[...]))

    @pl.when(j < nd)
    def _():
        v_ref[j] = act
        ssq_ref[...] += jnp.sum(act * act, axis=-1, keepdims=True)

    @pl.when(j >= nd)
    def _():
        grp = j - nd
        width = v_ref.shape[0] * v_ref.shape[2]
        r = lax.rsqrt(ssq_ref[...] * (1.0 / width) + NORM_EPS)
        vn = ((v_ref[grp] * r) * vg_ref[0]).astype(BF16)
        row = lax.broadcasted_iota(jnp.int32, (chunk, chunk), 0)
        col = lax.broadcasted_iota(jnp.int32, (chunk, chunk), 1)
        ws = jnp.where(row >= col, ws_ref[0], 0.0).astype(BF16)
        bias = bs_ref[0]
        parts = []
        for c in range(tm // chunk):
            parts.append(_dot(ws, vn[c * chunk:(c + 1) * chunk, :]) + bias)
        sv = jnp.concatenate(parts, axis=0)
        o_ref[...] += _dot((act * sv).astype(BF16), wo_ref[...])


def mixer_gmlp(x, g, w_uv, v_gain, w_s, b_s, w_out, *, tm):
    t, d = x.shape
    width = w_out.shape[0]
    tn = width // GMLP_GROUPS
    nd = GMLP_GROUPS
    chunk = GMLP_CHUNK
    assert tm % chunk == 0
    vg = v_gain.reshape(nd, 1, tn)
    bs = b_s.reshape(nd, chunk, 1)
    kern = functools.partial(_mixer_c_kernel, nd=nd, chunk=chunk)
    return pl.pallas_call(
        kern,
        grid=(t // tm, 2 * nd),
        in_specs=[_ANY,
                  pl.BlockSpec((1, d), lambda i, j: (0, 0)),
                  pl.BlockSpec((d, tn), lambda i, j: (0, jnp.where(j < nd, j + nd, j - nd))),
                  pl.BlockSpec((1, 1, tn), lambda i, j: (jnp.maximum(j - nd, 0), 0, 0)),
                  pl.BlockSpec((1, chunk, chunk), lambda i, j: (jnp.maximum(j - nd, 0), 0, 0)),
                  pl.BlockSpec((1, chunk, 1), lambda i, j: (jnp.maximum(j - nd, 0), 0, 0)),
                  pl.BlockSpec((tn, d), lambda i, j: (jnp.maximum(j - nd, 0), 0))],
        out_specs=pl.BlockSpec((tm, d), lambda i, j: (i, 0)),
        out_shape=jax.ShapeDtypeStruct((t, d), F32),
        scratch_shapes=[pltpu.VMEM((tm, d), BF16),
                        pltpu.VMEM((nd, tm, tn), F32),
                        pltpu.VMEM((tm, 1), F32)] + _XBUF(tm, d),
        compiler_params=_cparams(_SEQ2),
        name="mixer_gmlp",
    )(x, g, w_uv, vg, w_s, bs, w_out)


def _mixer_d_kernel(x_ref, xp_ref, g_ref, wa_ref, wg_ref, cw_ref, cb_ref, lg_ref, lb_ref,
                    wo_ref, o_ref, hn_ref, z_ref, mean_ref, rstd_ref,
                    *, nd, halo, tiles_per_seq):
    i = pl.program_id(0)
    j = pl.program_id(1)
    tm = x_ref.shape[0]
    width = z_ref.shape[0] * z_ref.shape[2]

    @pl.when(j == 0)
    def _():
        x = x_ref[...]
        g = g_ref[...]
        hn_ref[pl.ds(halo, tm), :] = _rmsnorm_f32(x, g).astype(BF16)
        hn_ref[pl.ds(0, halo), :] = _rmsnorm_f32(xp_ref[...], g).astype(BF16)
        o_ref[...] = x
        mean_ref[...] = jnp.zeros_like(mean_ref)

    @pl.when(j < nd)
    def _():
        hfull = hn_ref[...]
        a = _dot(hfull, wa_ref[...])
        gate = _dot(hfull, wg_ref[...])
        z = a * jax.nn.sigmoid(gate)
        keep = _halo_keep_mask(halo + tm, halo, i % tiles_per_seq == 0)
        z = jnp.where(keep, z, 0.0)
        kw = cw_ref.shape[0]
        acc = None
        for b in range(8):
            zb = _shift_rows(z, b)
            for s in range(b, kw, 8):
                term = zb[halo - (s - b):halo - (s - b) + tm, :] * cw_ref[kw - 1 - s:kw - s, :]
                acc = term if acc is None else acc + term
        acc = acc + cb_ref[0]
        z_ref[j] = acc
        mean_ref[...] += jnp.sum(acc, axis=-1, keepdims=True)

    @pl.when(j == nd)
    def _():
        mean = mean_ref[...] * (1.0 / width)
        mean_ref[...] = mean
        var = jnp.zeros_like(mean)
        for k in range(nd):
            c = z_ref[k] - mean
            var = var + jnp.sum(c * c, axis=-1, keepdims=True)
        rstd_ref[...] = lax.rsqrt(var * (1.0 / width) + NORM_EPS)

    @pl.when(j >= nd)
    def _():
        zn = (z_ref[j - nd] - mean_ref[...]) * rstd_ref[...]
        zn = zn * lg_ref[0] + lb_ref[0]
        act = (zn * jax.nn.sigmoid(zn)).astype(BF16)
        o_ref[...] += _dot(act, wo_ref[...])


def mixer_conformer(x, g, w_pw1, conv_w, conv_b, ln_g, ln_b, w_pw2, *, seq, tm, tn):
    t, d = x.shape
    nd = d // tn
    halo = HALO_D
    kw = conv_w.shape[0]
    assert kw - 1 <= halo and halo % 16 == 0
    hb = tm // halo
    lo = lambda j: jnp.minimum(j, nd - 1)
    hi = lambda j: jnp.maximum(j - nd, 0)
    kern = functools.partial(_mixer_d_kernel, nd=nd, halo=halo, tiles_per_seq=seq // tm)
    vec = lambda a: a.reshape(nd, 1, tn)
    return pl.pallas_call(
        kern,
        grid=(t // tm, 2 * nd),
        in_specs=[pl.BlockSpec((tm, d), lambda i, j: (i, 0)),
                  pl.BlockSpec((halo, d), lambda i, j: (jnp.maximum(i * hb - 1, 0), 0)),
                  pl.BlockSpec((1, d), lambda i, j: (0, 0)),
                  pl.BlockSpec((d, tn), lambda i, j: (0, lo(j))),
                  pl.BlockSpec((d, tn), lambda i, j: (0, nd + lo(j))),
                  pl.BlockSpec((kw, tn), lambda i, j: (0, lo(j))),
                  pl.BlockSpec((1, 1, tn), lambda i, j: (lo(j), 0, 0)),
                  pl.BlockSpec((1, 1, tn), lambda i, j: (hi(j), 0, 0)),
                  pl.BlockSpec((1, 1, tn), lambda i, j: (hi(j), 0, 0)),
                  pl.BlockSpec((tn, d), lambda i, j: (hi(j), 0))],
        out_specs=pl.BlockSpec((tm, d), lambda i, j: (i, 0)),
        out_shape=jax.ShapeDtypeStruct((t, d), F32),
        scratch_shapes=[pltpu.VMEM((halo + tm, d), BF16),
                        pltpu.VMEM((nd, tm, tn), F32),
                        pltpu.VMEM((tm, 1), F32),
                        pltpu.VMEM((tm, 1), F32)],
        compiler_params=_cparams(("parallel", "arbitrary")),
        name="mixer_conformer",
    )(x, x, g, w_pw1, w_pw1, conv_w, vec(conv_b), vec(ln_g), vec(ln_b), w_pw2)


def _router_kernel(x_ref, g_ref, rt_ref, e1_ref, e2_ref, g1_ref, g2_ref, r1_ref, r2_ref,
                   cnt_ref, carry_ref, *, n_experts):
    tm = x_ref.shape[0]

    @pl.when(pl.program_id(0) == 0)
    def _():
        carry_ref[...] = jnp.zeros_like(carry_ref)

    h = _rmsnorm_f32(x_ref[...], g_ref[...])
    logits = [jnp.sum(h * rt_ref[e:e + 1, :], axis=-1, keepdims=True) for e in range(n_experts)]
    m1 = logits[0]
    i1 = jnp.zeros((tm, 1), jnp.int32)
    for e in range(1, n_experts):
        better = logits[e] > m1
        m1 = jnp.where(better, logits[e], m1)
        i1 = jnp.where(better, e, i1)
    m2 = jnp.full((tm, 1), -jnp.inf, F32)
    i2 = jnp.where(i1 == 0, 1, 0).astype(jnp.int32)
    for e in range(n_experts):
        better = jnp.logical_and(i1 != e, logits[e] > m2)
        m2 = jnp.where(better, logits[e], m2)
        i2 = jnp.where(better, e, i2)
    ex = jnp.exp(m2 - m1)
    denom = 1.0 + ex
    e1_ref[...] = i1
    e2_ref[...] = i2
    g1_ref[...] = 1.0 / denom
    g2_ref[...] = ex / denom

    lane = lax.broadcasted_iota(jnp.int32, (tm, V7X_LANES), 1)
    hit1 = lane == i1
    hit2 = lane == i2
    sel = jnp.logical_or(hit1, hit2).astype(BF16)
    row = lax.broadcasted_iota(jnp.int32, (tm, tm), 0)
    col = lax.broadcasted_iota(jnp.int32, (tm, tm), 1)
    before = (col < row).astype(BF16)
    ranks = _dot(before, sel) + carry_ref[...]
    r1_ref[...] = jnp.sum(jnp.where(hit1, ranks, 0.0), axis=-1, keepdims=True).astype(jnp.int32)
    r2_ref[...] = jnp.sum(jnp.where(hit2, ranks, 0.0), axis=-1, keepdims=True).astype(jnp.int32)
    total = carry_ref[...] + jnp.sum(sel.astype(F32), axis=0, keepdims=True)
    carry_ref[...] = total
    cnt_ref[...] = jnp.broadcast_to(total, cnt_ref.shape).astype(jnp.int32)


def moe_router(x, g, router_t, *, tm):
    t, d = x.shape
    n_experts = router_t.shape[0]
    col = lambda dt: jax.ShapeDtypeStruct((t, 1), dt)
    cspec = pl.BlockSpec((tm, 1), lambda i: (i, 0))
    return pl.pallas_call(
        functools.partial(_router_kernel, n_experts=n_experts),
        grid=(t // tm,),
        in_specs=[pl.BlockSpec((tm, d), lambda i: (i, 0)),
                  pl.BlockSpec((1, d), lambda i: (0, 0)),
                  pl.BlockSpec((n_experts, d), lambda i: (0, 0))],
        out_specs=[cspec, cspec, cspec, cspec, cspec, cspec,
                   pl.BlockSpec((8, V7X_LANES), lambda i: (0, 0))],
        out_shape=[col(jnp.int32), col(jnp.int32), col(F32), col(F32),
                   col(jnp.int32), col(jnp.int32),
                   jax.ShapeDtypeStruct((8, V7X_LANES), jnp.int32)],
        scratch_shapes=[pltpu.VMEM((1, V7X_LANES), F32)],
        compiler_params=_cparams(("arbitrary",)),
        name="moe_router",
    )(x, g, router_t)


def _row_copy_wait(src_ref, dst_ref, sem, n_rows_src):
    pltpu.make_async_copy(src_ref, dst_ref.at[pl.ds(0, n_rows_src)], sem).wait()


ISSUE_UNROLL = 8


def _dispatch_kernel(p1_ref, p2_ref, zf_ref, x_ref, g_ref, xs_ref, h_ref, sem, zsem, *, tm_moe):
    i = pl.program_id(0)
    tm = x_ref.shape[0]
    n_tiles = xs_ref.shape[0] // tm_moe

    @pl.when(i == 0)
    def _():
        h_ref[...] = jnp.zeros_like(h_ref)

        def fill_copy(k, c):
            start = pl.multiple_of(k * tm_moe + c * tm, tm)
            return pltpu.make_async_copy(h_ref, xs_ref.at[pl.ds(start, tm)], zsem)

        def start_fill(k, carry):
            @pl.when(zf_ref[k] == 1)
            def _():
                for c in range(tm_moe // tm):
                    fill_copy(k, c).start()
            return carry

        def wait_fill(k, carry):
            @pl.when(zf_ref[k] == 1)
            def _():
                for c in range(tm_moe // tm):
                    fill_copy(k, c).wait()
            return carry

        lax.fori_loop(0, n_tiles, start_fill, 0)
        lax.fori_loop(0, n_tiles, wait_fill, 0)

    h_ref[...] = _rmsnorm_f32(x_ref[...], g_ref[...])

    def issue(r, carry):
        t = i * tm + r
        src = h_ref.at[pl.ds(r, 1)]
        pltpu.make_async_copy(src, xs_ref.at[pl.ds(p1_ref[t], 1)], sem).start()
        pltpu.make_async_copy(src, xs_ref.at[pl.ds(p2_ref[t], 1)], sem).start()
        return carry

    lax.fori_loop(0, tm, issue, 0, unroll=ISSUE_UNROLL)
    for _ in range(TOP_K):
        _row_copy_wait(h_ref, xs_ref, sem, tm)


def moe_dispatch(x, g, pos1, pos2, zero_fill, n_rows, *, tm, tm_moe):
    t, d = x.shape
    assert tm_moe % tm == 0
    grid_spec = pltpu.PrefetchScalarGridSpec(
        num_scalar_prefetch=3,
        grid=(t // tm,),
        in_specs=[pl.BlockSpec((tm, d), lambda i, p1, p2, zf: (i, 0)),
                  pl.BlockSpec((1, d), lambda i, p1, p2, zf: (0, 0))],
        out_specs=_ANY,
        scratch_shapes=[pltpu.VMEM((tm, d), F32), pltpu.SemaphoreType.DMA(()),
                        pltpu.SemaphoreType.DMA(())],
    )
    return pl.pallas_call(
        functools.partial(_dispatch_kernel, tm_moe=tm_moe),
        grid_spec=grid_spec,
        out_shape=jax.ShapeDtypeStruct((n_rows, d), F32),
        compiler_params=_cparams(("arbitrary",)),
        name="moe_dispatch",
    )(pos1, pos2, zero_fill, x, g)


def _moe_ffn_kernel(te_ref, xs_hbm, w1_ref, w3_ref, w2_ref, ys_ref, hn_ref, xbuf, xsem,
                    *, n_tiles):
    i = pl.program_id(0)
    n_active = te_ref[n_tiles]
    active = i < n_active

    @pl.when(pl.program_id(1) == 0)
    def _():
        ys_ref[...] = jnp.zeros_like(ys_ref)

        @pl.when(active)
        def _():
            _await_tile(xs_hbm, xbuf, xsem, i)
            hn_ref[...] = xbuf[...].astype(BF16)
            _request_tile(xs_hbm, xbuf, xsem, i + 1, n_active)

    @pl.when(active)
    def _():
        hn = hn_ref[...]
        a = _dot(hn, w1_ref[...])
        b = _dot(hn, w3_ref[...])
        act = (a * jax.nn.sigmoid(a) * b).astype(BF16)
        ys_ref[...] += _dot(act, w2_ref[...])


def moe_grouped_ffn(xs, tile_expert, w1, w3, w2, *, tm, tf):
    n_rows, d = xs.shape
    f = w1.shape[2]
    n_tiles = n_rows // tm
    nf = f // tf

    def jj(i, j, te):
        return jnp.where(i < te[n_tiles], j, nf - 1)

    grid_spec = pltpu.PrefetchScalarGridSpec(
        num_scalar_prefetch=1,
        grid=(n_tiles, nf),
        in_specs=[_ANY,
                  pl.BlockSpec((None, d, tf), lambda i, j, te: (te[i], 0, jj(i, j, te))),
                  pl.BlockSpec((None, d, tf), lambda i, j, te: (te[i], 0, jj(i, j, te))),
                  pl.BlockSpec((None, tf, d), lambda i, j, te: (te[i], jj(i, j, te), 0))],
        out_specs=pl.BlockSpec((tm, d), lambda i, j, te: (i, 0)),
        scratch_shapes=[pltpu.VMEM((tm, d), BF16)] + _XBUF(tm, d),
    )
    return pl.pallas_call(
        functools.partial(_moe_ffn_kernel, n_tiles=n_tiles),
        grid_spec=grid_spec,
        out_shape=jax.ShapeDtypeStruct((n_rows, d), F32),
        compiler_params=_cparams(("arbitrary", "arbitrary")),
        name="moe_grouped_ffn",
    )(tile_expert, xs, w1, w3, w2)


def _combine_kernel(p1_ref, p2_ref, x_ref, g1_ref, g2_ref, ys_ref, *rest, final_norm):
    if final_norm:
        fg_ref, o_ref, a_ref, b_ref, sem = rest
    else:
        o_ref, a_ref, b_ref, sem = rest
    i = pl.program_id(0)
    tm = x_ref.shape[0]

    def issue(r, carry):
        t = i * tm + r
        pltpu.make_async_copy(ys_ref.at[pl.ds(p1_ref[t], 1)], a_ref.at[pl.ds(r, 1)], sem).start()
        pltpu.make_async_copy(ys_ref.at[pl.ds(p2_ref[t], 1)], b_ref.at[pl.ds(r, 1)], sem).start()
        return carry

    lax.fori_loop(0, tm, issue, 0, unroll=ISSUE_UNROLL)
    _row_copy_wait(a_ref, ys_ref, sem, tm)
    _row_copy_wait(b_ref, ys_ref, sem, tm)
    y = x_ref[...] + (g1_ref[...] * a_ref[...] + g2_ref[...] * b_ref[...])
    if final_norm:
        y = _rmsnorm_f32(y, fg_ref[...])
    o_ref[...] = y


def moe_combine(x, g1, g2, ys, pos1, pos2, final_g, *, tm):
    t, d = x.shape
    final_norm = final_g is not None
    in_specs = [pl.BlockSpec((tm, d), lambda i, p1, p2: (i, 0)),
                pl.BlockSpec((tm, 1), lambda i, p1, p2: (i, 0)),
                pl.BlockSpec((tm, 1), lambda i, p1, p2: (i, 0)),
                pl.BlockSpec(memory_space=pl.ANY)]
    args = [pos1, pos2, x, g1, g2, ys]
    if final_norm:
        in_specs.append(pl.BlockSpec((1, d), lambda i, p1, p2: (0, 0)))
        args.append(final_g)
    grid_spec = pltpu.PrefetchScalarGridSpec(
        num_scalar_prefetch=2,
        grid=(t // tm,),
        in_specs=in_specs,
        out_specs=pl.BlockSpec((tm, d), lambda i, p1, p2: (i, 0)),
        scratch_shapes=[pltpu.VMEM((tm, d), F32), pltpu.VMEM((tm, d), F32),
                        pltpu.SemaphoreType.DMA(())],
    )
    return pl.pallas_call(
        functools.partial(_combine_kernel, final_norm=final_norm),
        grid_spec=grid_spec,
        out_shape=jax.ShapeDtypeStruct((t, d), F32),
        compiler_params=_cparams(("arbitrary",)),
        name="moe_combine",
    )(*args)


def moe_ffn(x, g, router, w1, w3, w2, final_g, *, tm_moe, tf):
    t, d = x.shape
    n_experts = router.shape[1]
    e1, e2, g1, g2, r1, r2, cnt = moe_router(x, g, router.T, tm=TM_ROW)

    counts = cnt[0, :n_experts]
    padded = ((counts + tm_moe - 1) // tm_moe) * tm_moe
    ends = jnp.cumsum(padded)
    starts = ends - padded
    pos1 = (starts[e1[:, 0]] + r1[:, 0]).astype(jnp.int32)
    pos2 = (starts[e2[:, 0]] + r2[:, 0]).astype(jnp.int32)
    n_rows = TOP_K * t + n_experts * tm_moe
    n_tiles = n_rows // tm_moe
    n_active = (ends[-1] // tm_moe).astype(jnp.int32)
    tile_row = jnp.arange(n_tiles, dtype=jnp.int32) * tm_moe
    tile_e = jnp.sum((tile_row[:, None] >= ends[None, :]).astype(jnp.int32), axis=1)
    tile_e = jnp.minimum(tile_e, n_experts - 1)
    last_e = tile_e[jnp.maximum(n_active - 1, 0)]
    tile_e = jnp.where(jnp.arange(n_tiles) < n_active, tile_e, last_e)
    tile_expert = jnp.concatenate([tile_e, n_active[None]]).astype(jnp.int32)
    group_tail = jnp.any((tile_row[:, None] + tm_moe == ends[None, :]) & (padded[None, :] > 0),
                         axis=1)
    zero_fill = (group_tail | (jnp.arange(n_tiles) >= n_active)).astype(jnp.int32)

    xs = moe_dispatch(x, g, pos1, pos2, zero_fill, n_rows, tm=TM_ROW, tm_moe=tm_moe)
    ys = moe_grouped_ffn(xs, tile_expert, w1, w3, w2, tm=tm_moe, tf=tf)
    return moe_combine(x, g1, g2, ys, pos1, pos2, final_g, tm=TM_ROW)


def kernel(x, mem, mem_norm, final_norm, mixer_norm, xattn_norm, ffn_norm, xa_wq, xa_wk, xa_wv, xa_wo, a_w_in, a_conv, a_w_out, b_w_map, b_scale, c_w_uv, c_v_norm, c_w_s, c_b_s, c_w_out, d_w_pw1, d_conv, d_conv_b, d_ln_g, d_ln_b, d_w_pw2, f_w1, f_w3, f_w2, m_router, m_w1, m_w3, m_w2):
    batch, seq, d = x.shape
    depth = mixer_norm.shape[0]
    n_mixers = 4
    bf = lambda w: w.astype(BF16)
    row = lambda v: v.reshape(1, -1)

    mem2 = mem.reshape(batch * mem.shape[1], d)
    wkv = jnp.concatenate([bf(w) for l in range(depth) for w in (xa_wk[l], xa_wv[l])], axis=1)
    kv = rms_matmul(mem2, row(mem_norm), wkv, tm=mem2.shape[0], tn=TN)
    kv = kv.reshape(batch, mem.shape[1], 2 * depth * d)

    xt = x.reshape(batch * seq, d)
    for i in range(depth):
        mixer, j = i % n_mixers, i // n_mixers
        g = row(mixer_norm[i])
        if mixer == 0:
            xt = mixer_short_conv(xt, g, bf(a_w_in[j]), a_conv[j], bf(a_w_out[j]),
                                  seq=seq, tm=TM, tn=TN)
        elif mixer == 1:
            xt = mixer_pooling(xt, g, bf(b_w_map[j]), row(b_scale[j]), seq=seq, tm=TM_VPU)
        elif mixer == 2:
            xt = mixer_gmlp(xt, g, bf(c_w_uv[j]), c_v_norm[j], c_w_s[j], c_b_s[j],
                            bf(c_w_out[j]), tm=TM)
        else:
            xt = mixer_conformer(xt, g, bf(d_w_pw1[j]), d_conv[j], d_conv_b[j], d_ln_g[j],
                                 d_ln_b[j], bf(d_w_pw2[j]), seq=seq, tm=TM_VPU, tn=TN)
        xt = cross_attention(xt, row(xattn_norm[i]), bf(xa_wq[i]), kv, i, bf(xa_wo[i]),
                             seq=seq, tm=TM)
        k = i // 2
        gf = row(ffn_norm[i])
        if i % 2 == 0:
            xt = swiglu_ffn(xt, gf, bf(f_w1[k]), bf(f_w3[k]), bf(f_w2[k]), tm=TM, tf=TN)
        else:
            fin = row(final_norm) if i == depth - 1 else None
            xt = moe_ffn(xt, gf, m_router[k], bf(m_w1[k]), bf(m_w3[k]), bf(m_w2[k]), fin,
                         tm_moe=TM_MOE, tf=TN)
    if depth % 2 == 1:
        raise NotImplementedError("final RMSNorm is fused into the last (expert) layer")
    return xt.reshape(batch, seq, d)
```

```python
import functools

import jax
import jax.numpy as jnp
from jax import lax
from jax.experimental import pallas as pl
from jax.experimental.pallas import tpu as pltpu

NORM_EPS = 1e-6
XATTN_HEADS = 4
POOL_WINDOWS = (2, 4, 8, 16)
GMLP_CHUNK = 128
GMLP_GROUPS = 4
TOP_K = 2

BF16 = jnp.bfloat16
F32 = jnp.float32

V7X_VMEM_BYTES = 64 * 1024 * 1024
V7X_LANES = 128
VMEM_LIMIT = 56 * 1024 * 1024

TM = 1024
TM_VPU = 512
TN = 512
TM_MOE = 1024
TM_ROW = 512
TM_XATTN = 512
CAST_ROWS = 256
CONV_ROWS = 64
HALO_A = 16
HALO_B = 16
HALO_D = 32


def _cparams(semantics):
    return pltpu.CompilerParams(dimension_semantics=semantics,
                                vmem_limit_bytes=VMEM_LIMIT)


def _rms_scale(x):
    return lax.rsqrt(jnp.mean(x * x, axis=-1, keepdims=True) + NORM_EPS)


def _rmsnorm_f32(x, g):
    return (x * _rms_scale(x)) * g


def _dot(a, b):
    return jnp.dot(a, b, preferred_element_type=F32)


def _shift_rows(z, s):
    if s == 0:
        return z
    return pltpu.roll(z, s, axis=0)


def _tile_copy(src_hbm, buf, sem, tile):
    tm = buf.shape[0]
    start = pl.multiple_of(tile * tm, tm)
    return pltpu.make_async_copy(src_hbm.at[pl.ds(start, tm)], buf, sem)


def _await_tile(src_hbm, buf, sem, i):
    @pl.when(i == 0)
    def _():
        _tile_copy(src_hbm, buf, sem, 0).start()

    _tile_copy(src_hbm, buf, sem, i).wait()


def _request_tile(src_hbm, buf, sem, nxt, n_tiles):
    @pl.when(nxt < n_tiles)
    def _():
        _tile_copy(src_hbm, buf, sem, nxt).start()


_XBUF = lambda tm, d: [pltpu.VMEM((tm, d), F32), pltpu.SemaphoreType.DMA(())]
_ANY = pl.BlockSpec(memory_space=pl.ANY)
_SEQ2 = ("arbitrary", "arbitrary")


def _halo_keep_mask(rows, halo, first_tile):
    r = lax.broadcasted_iota(jnp.int32, (rows, 1), 0)
    return jnp.logical_or(r >= halo, jnp.logical_not(first_tile))


def _cast_kernel(w_ref, o_ref):
    o_ref[...] = w_ref[...].astype(o_ref.dtype)


def cast_bf16(w, first=0, count=None):
    r, c = w.shape[-2:]
    w3 = w.reshape(-1, r, c)
    count = w3.shape[0] - first if count is None else count
    br = CAST_ROWS if r % CAST_ROWS == 0 else r
    return pl.pallas_call(
        _cast_kernel,
        grid=(count, r // br),
        in_specs=[pl.BlockSpec((None, br, c), lambda e, i: (first + e, i, 0))],
        out_specs=pl.BlockSpec((None, br, c), lambda e, i: (e, i, 0)),
        out_shape=jax.ShapeDtypeStruct((count, r, c), BF16),
        compiler_params=_cparams(("parallel", "parallel")),
        name="cast_bf16",
    )(w3)


def _rms_mm_kernel(x_ref, g_ref, w_ref, o_ref, hn_ref):
    @pl.when(pl.program_id(1) == 0)
    def _():
        hn_ref[...] = _rmsnorm_f32(x_ref[...], g_ref[...]).astype(BF16)

    o_ref[...] = _dot(hn_ref[...], w_ref[...]).astype(o_ref.dtype)


def rms_matmul_layers(x, g, w, *, tm, tn):
    m, d = x.shape
    layers, _, n = w.shape
    nb = n // tn
    return pl.pallas_call(
        _rms_mm_kernel,
        grid=(m // tm, layers * nb),
        in_specs=[pl.BlockSpec((tm, d), lambda i, j: (i, 0)),
                  pl.BlockSpec((1, d), lambda i, j: (0, 0)),
                  pl.BlockSpec((None, d, tn), lambda i, j: (j // nb, 0, j % nb))],
        out_specs=pl.BlockSpec((tm, tn), lambda i, j: (i, j)),
        out_shape=jax.ShapeDtypeStruct((m, layers * n), BF16),
        scratch_shapes=[pltpu.VMEM((tm, d), BF16)],
        compiler_params=_cparams(("parallel", "arbitrary")),
        name="mem_kv_proj",
    )(x, g, w)


def _xattn_kernel(x_ref, g_ref, wq_ref, k_ref, v_ref, wo_ref, o_ref, q_ref, oh_ref,
                  *, scale, heads):
    x = x_ref[...]
    hn = _rmsnorm_f32(x, g_ref[...]).astype(BF16)
    q_ref[...] = _dot(hn, wq_ref[...]).astype(BF16)
    dh = wq_ref.shape[1] // heads
    for h in range(heads):
        cols = slice(h * dh, (h + 1) * dh)
        s = lax.dot_general(q_ref[:, cols], k_ref[:, cols], (((1,), (1,)), ((), ())),
                            preferred_element_type=F32) * scale
        e = jnp.exp(s - jnp.max(s, axis=-1, keepdims=True))
        p = (e / jnp.sum(e, axis=-1, keepdims=True)).astype(BF16)
        oh_ref[:, cols] = _dot(p, v_ref[:, cols]).astype(BF16)
    o_ref[...] = x + _dot(oh_ref[...], wo_ref[...])


def cross_attention(x, g, wq, k_all, v_all, layer, wo, *, seq, tm):
    t, d = x.shape
    mem = k_all.shape[1]
    tiles_per_seq = seq // tm
    resident = pl.Buffered(1)
    kv_spec = pl.BlockSpec((None, mem, d), lambda i: (i // tiles_per_seq, 0, layer))
    return pl.pallas_call(
        functools.partial(_xattn_kernel, scale=float(d // XATTN_HEADS) ** -0.5,
                          heads=XATTN_HEADS),
        grid=(t // tm,),
        in_specs=[pl.BlockSpec((tm, d), lambda i: (i, 0)),
                  pl.BlockSpec((1, d), lambda i: (0, 0)),
                  pl.BlockSpec((None, d, d), lambda i: (layer, 0, 0), pipeline_mode=resident),
                  kv_spec, kv_spec,
                  pl.BlockSpec((None, d, d), lambda i: (layer, 0, 0), pipeline_mode=resident)],
        out_specs=pl.BlockSpec((tm, d), lambda i: (i, 0)),
        out_shape=jax.ShapeDtypeStruct((t, d), F32),
        scratch_shapes=[pltpu.VMEM((tm, d), BF16), pltpu.VMEM((tm, d), BF16)],
        compiler_params=_cparams(("parallel",)),
        name="cross_attention",
    )(x, g, wq, k_all, v_all, wo)


def _ffn_kernel(x_hbm, g_ref, w1_ref, w3_ref, w2_ref, o_ref, hn_ref, xbuf, xsem):
    i = pl.program_id(0)

    @pl.when(pl.program_id(1) == 0)
    def _():
        _await_tile(x_hbm, xbuf, xsem, i)
        x = xbuf[...]
        hn_ref[...] = _rmsnorm_f32(x, g_ref[...]).astype(BF16)
        o_ref[...] = x
        _request_tile(x_hbm, xbuf, xsem, i + 1, pl.num_programs(0))

    hn = hn_ref[...]
    a = _dot(hn, w1_ref[...])
    b = _dot(hn, w3_ref[...])
    act = (a * jax.nn.sigmoid(a) * b).astype(BF16)
    o_ref[...] += _dot(act, w2_ref[...])


def swiglu_ffn(x, g, w1, w3, w2, *, tm, tf):
    t, d = x.shape
    f = w1.shape[1]
    return pl.pallas_call(
        _ffn_kernel,
        grid=(t // tm, f // tf),
        in_specs=[_ANY,
                  pl.BlockSpec((1, d), lambda i, j: (0, 0)),
                  pl.BlockSpec((d, tf), lambda i, j: (0, j)),
                  pl.BlockSpec((d, tf), lambda i, j: (0, j)),
                  pl.BlockSpec((tf, d), lambda i, j: (j, 0))],
        out_specs=pl.BlockSpec((tm, d), lambda i, j: (i, 0)),
        out_shape=jax.ShapeDtypeStruct((t, d), F32),
        scratch_shapes=[pltpu.VMEM((tm, d), BF16)] + _XBUF(tm, d),
        compiler_params=_cparams(_SEQ2),
        name="swiglu_ffn",
    )(x, g, w1, w3, w2)


def _mixer_a_kernel(x_hbm, xp_ref, g_ref, wb_ref, wc_ref, wz_ref, cw_ref, wo_ref,
                    o_ref, hn_ref, xbuf, xsem, *, halo, tiles_per_seq):
    i = pl.program_id(0)
    tm = o_ref.shape[0]

    @pl.when(pl.program_id(1) == 0)
    def _():
        _await_tile(x_hbm, xbuf, xsem, i)
        x = xbuf[...]
        g = g_ref[...]
        hn_ref[pl.ds(halo, tm), :] = _rmsnorm_f32(x, g).astype(BF16)
        hn_ref[pl.ds(0, halo), :] = _rmsnorm_f32(xp_ref[...], g).astype(BF16)
        o_ref[...] = x
        _request_tile(x_hbm, xbuf, xsem, i + 1, pl.num_programs(0))

    hfull = hn_ref[...]
    bgate = _dot(hn_ref[pl.ds(halo, tm), :], wb_ref[...])
    u = _dot(hfull, wc_ref[...]) * _dot(hfull, wz_ref[...])
    keep = _halo_keep_mask(halo + tm, halo, i % tiles_per_seq == 0)
    u = jnp.where(keep, u, 0.0)
    kw = cw_ref.shape[0]
    conv = u * cw_ref[kw - 1:kw, :]
    for s in range(1, kw):
        conv = conv + _shift_rows(u, s) * cw_ref[kw - 1 - s:kw - s, :]
    y = (bgate * conv[halo:, :]).astype(BF16)
    o_ref[...] += _dot(y, wo_ref[...])


def mixer_short_conv(x, g, w_in, conv_w, w_out, *, seq, tm, tn):
    t, d = x.shape
    nd = d // tn
    halo = HALO_A
    assert conv_w.shape[0] - 1 <= halo
    hb = tm // halo
    kern = functools.partial(_mixer_a_kernel, halo=halo, tiles_per_seq=seq // tm)
    return pl.pallas_call(
        kern,
        grid=(t // tm, nd),
        in_specs=[_ANY,
                  pl.BlockSpec((halo, d), lambda i, j: (jnp.maximum(i * hb - 1, 0), 0)),
                  pl.BlockSpec((1, d), lambda i, j: (0, 0)),
                  pl.BlockSpec((d, tn), lambda i, j: (0, j)),
                  pl.BlockSpec((d, tn), lambda i, j: (0, nd + j)),
                  pl.BlockSpec((d, tn), lambda i, j: (0, 2 * nd + j)),
                  pl.BlockSpec((conv_w.shape[0], tn), lambda i, j: (0, j)),
                  pl.BlockSpec((tn, d), lambda i, j: (j, 0))],
        out_specs=pl.BlockSpec((tm, d), lambda i, j: (i, 0)),
        out_shape=jax.ShapeDtypeStruct((t, d), F32),
        scratch_shapes=[pltpu.VMEM((halo + tm, d), BF16)] + _XBUF(tm, d),
        compiler_params=_cparams(_SEQ2),
        name="mixer_short_conv",
    )(x, x, g, w_in, w_in, w_in, conv_w, w_out)


def _mixer_b_kernel(x_ref, xp_ref, g_ref, wm_ref, sc_ref, o_ref, *, halo, tiles_per_seq, windows):
    i = pl.program_id(0)
    tm, d = x_ref.shape
    dg = d // len(windows)
    x = x_ref[...]
    g = g_ref[...]
    first = i % tiles_per_seq == 0
    h_cur = _rmsnorm_f32(x, g)
    h_prev = jnp.where(first, 0.0, _rmsnorm_f32(xp_ref[...], g))
    h = jnp.concatenate([h_prev, h_cur], axis=0)
    pos = (i % tiles_per_seq) * tm + lax.broadcasted_iota(jnp.int32, (tm, 1), 0)
    for gi, w in enumerate(windows):
        hg = h[:, gi * dg:(gi + 1) * dg]
        wsum = hg
        span = 1
        while span < w:
            wsum = wsum + _shift_rows(wsum, span)
            span *= 2
        count = jnp.minimum(pos + 1, w).astype(F32)
        pooled = wsum[halo:, :] / count - hg[halo:, :]
        y = _dot(pooled.astype(BF16), wm_ref[gi]) * sc_ref[:, gi * dg:(gi + 1) * dg]
        o_ref[:, gi * dg:(gi + 1) * dg] = x[:, gi * dg:(gi + 1) * dg] + y


def mixer_pooling(x, g, w_map, scale, *, seq, tm):
    t, d = x.shape
    halo = HALO_B
    assert max(POOL_WINDOWS) <= halo and all(w & (w - 1) == 0 for w in POOL_WINDOWS)
    hb = tm // halo
    kern = functools.partial(_mixer_b_kernel, halo=halo, tiles_per_seq=seq // tm,
                             windows=POOL_WINDOWS)
    return pl.pallas_call(
        kern,
        grid=(t // tm,),
        in_specs=[pl.BlockSpec((tm, d), lambda i: (i, 0)),
                  pl.BlockSpec((halo, d), lambda i: (jnp.maximum(i * hb - 1, 0), 0)),
                  pl.BlockSpec((1, d), lambda i: (0, 0)),
                  pl.BlockSpec(w_map.shape, lambda i: (0, 0, 0)),
                  pl.BlockSpec((1, d), lambda i: (0, 0))],
        out_specs=pl.BlockSpec((tm, d), lambda i: (i, 0)),
        out_shape=jax.ShapeDtypeStruct((t, d), F32),
        compiler_params=_cparams(("parallel",)),
        name="mixer_pooling",
    )(x, x, g, w_map, scale)


def _gelu(x):
    return 0.5 * x * (1.0 + lax.erf(x * (2.0 ** -0.5)))


def _mixer_c_kernel(x_hbm, g_ref, wuv_ref, vg_ref, ws_ref, bs_ref, wo_ref,
                    o_ref, hn_ref, v_ref, ssq_ref, xbuf, xsem, *, nd, chunk):
    i = pl.program_id(0)
    j = pl.program_id(1)
    tm = o_ref.shape[0]

    @pl.when(j == 0)
    def _():
        _await_tile(x_hbm, xbuf, xsem, i)
        x = xbuf[...]
        hn_ref[...] = _rmsnorm_f32(x, g_ref[...]).astype(BF16)
        o_ref[...] = x
        ssq_ref[...] = jnp.zeros_like(ssq_ref)
        _request_tile(x_hbm, xbuf, xsem, i + 1, pl.num_programs(0))

    act = _gelu(_dot(hn_ref[...], wuv_ref[...]))

    @pl.when(j < nd)
    def _():
        v_ref[j] = act
        ssq_ref[...] += jnp.sum(act * act, axis=-1, keepdims=True)

    @pl.when(j >= nd)
    def _():
        grp = j - nd
        width = v_ref.shape[0] * v_ref.shape[2]
        r = lax.rsqrt(ssq_ref[...] * (1.0 / width) + NORM_EPS)
        vn = ((v_ref[grp] * r) * vg_ref[0]).astype(BF16)
        row = lax.broadcasted_iota(jnp.int32, (chunk, chunk), 0)
        col = lax.broadcasted_iota(jnp.int32, (chunk, chunk), 1)
        ws = jnp.where(row >= col, ws_ref[0], 0.0).astype(BF16)
        bias = bs_ref[0]
        parts = []
        for c in range(tm // chunk):
            parts.append(_dot(ws, vn[c * chunk:(c + 1) * chunk, :]) + bias)
        sv = jnp.concatenate(parts, axis=0)
        o_ref[...] += _dot((act * sv).astype(BF16), wo_ref[...])


def mixer_gmlp(x, g, w_uv, v_gain, w_s, b_s, w_out, *, tm):
    t, d = x.shape
    width = w_out.shape[0]
    tn = width // GMLP_GROUPS
    nd = GMLP_GROUPS
    chunk = GMLP_CHUNK
    assert tm % chunk == 0
    vg = v_gain.reshape(nd, 1, tn)
    bs = b_s.reshape(nd, chunk, 1)
    kern = functools.partial(_mixer_c_kernel, nd=nd, chunk=chunk)
    return pl.pallas_call(
        kern,
        grid=(t // tm, 2 * nd),
        in_specs=[_ANY,
                  pl.BlockSpec((1, d), lambda i, j: (0, 0)),
                  pl.BlockSpec((d, tn), lambda i, j: (0, jnp.where(j < nd, j + nd, j - nd))),
                  pl.BlockSpec((1, 1, tn), lambda i, j: (jnp.maximum(j - nd, 0), 0, 0)),
                  pl.BlockSpec((1, chunk, chunk), lambda i, j: (jnp.maximum(j - nd, 0), 0, 0)),
                  pl.BlockSpec((1, chunk, 1), lambda i, j: (jnp.maximum(j - nd, 0), 0, 0)),
                  pl.BlockSpec((tn, d), lambda i, j: (jnp.maximum(j - nd, 0), 0))],
        out_specs=pl.BlockSpec((tm, d), lambda i, j: (i, 0)),
        out_shape=jax.ShapeDtypeStruct((t, d), F32),
        scratch_shapes=[pltpu.VMEM((tm, d), BF16),
                        pltpu.VMEM((nd, tm, tn), F32),
                        pltpu.VMEM((tm, 1), F32)] + _XBUF(tm, d),
        compiler_params=_cparams(_SEQ2),
        name="mixer_gmlp",
    )(x, g, w_uv, vg, w_s, bs, w_out)


def _mixer_d_kernel(x_ref, xp_ref, g_ref, wa_ref, wg_ref, cw_ref, cb_ref, lg_ref, lb_ref,
                    wo_ref, o_ref, hn_ref, z_ref, mean_ref, rstd_ref, stage_ref,
                    *, nd, halo, tiles_per_seq):
    i = pl.program_id(0)
    j = pl.program_id(1)
    tm = x_ref.shape[0]
    width = z_ref.shape[0] * z_ref.shape[2]

    @pl.when(j == 0)
    def _():
        x = x_ref[...]
        g = g_ref[...]
        hn_ref[pl.ds(halo, tm), :] = _rmsnorm_f32(x, g).astype(BF16)
        hn_ref[pl.ds(0, halo), :] = _rmsnorm_f32(xp_ref[...], g).astype(BF16)
        o_ref[...] = x
        mean_ref[...] = jnp.zeros_like(mean_ref)

    def glu_into_stage(slot):
        hfull = hn_ref[...]
        z = _dot(hfull, wa_ref[...]) * jax.nn.sigmoid(_dot(hfull, wg_ref[...]))
        keep = _halo_keep_mask(halo + tm, halo, i % tiles_per_seq == 0)
        stage_ref[slot] = jnp.where(keep, z, 0.0)

    def conv_from_stage(slot, tile):
        kw = cw_ref.shape[0]
        rows = CONV_ROWS
        for r0 in range(0, tm, rows):
            rowsum = None
            for l0 in range(0, cw_ref.shape[1], V7X_LANES):
                lanes = slice(l0, l0 + V7X_LANES)
                win = stage_ref[slot, r0:r0 + halo + rows, lanes]
                acc = None
                for b in range(8):
                    wb = _shift_rows(win, b)
                    for s in range(b, kw, 8):
                        off = halo - (s - b)
                        term = wb[off:off + rows, :] * cw_ref[kw - 1 - s:kw - s, lanes]
                        acc = term if acc is None else acc + term
                acc = acc + cb_ref[0][:, lanes]
                z_ref[tile, r0:r0 + rows, lanes] = acc
                part = jnp.sum(acc, axis=-1, keepdims=True)
                rowsum = part if rowsum is None else rowsum + part
            mean_ref[r0:r0 + rows, :] += rowsum

    @pl.when(j == 0)
    def _():
        glu_into_stage(0)

    @pl.when(jnp.logical_and(j >= 1, j < nd))
    def _():
        conv_from_stage((j - 1) % 2, j - 1)
        glu_into_stage(j % 2)

    @pl.when(j == nd)
    def _():
        conv_from_stage((nd - 1) % 2, nd - 1)
        mean = mean_ref[...] * (1.0 / width)
        mean_ref[...] = mean
        var = jnp.zeros_like(mean)
        for k in range(nd):
            c = z_ref[k] - mean
            var = var + jnp.sum(c * c, axis=-1, keepdims=True)
        rstd_ref[...] = lax.rsqrt(var * (1.0 / width) + NORM_EPS)

    @pl.when(j >= nd)
    def _():
        zn = (z_ref[j - nd] - mean_ref[...]) * rstd_ref[...]
        zn = zn * lg_ref[0] + lb_ref[0]
        act = (zn * jax.nn.sigmoid(zn)).astype(BF16)
        o_ref[...] += _dot(act, wo_ref[...])


def mixer_conformer(x, g, w_pw1, conv_w, conv_b, ln_g, ln_b, w_pw2, *, seq, tm, tn):
    t, d = x.shape
    nd = d // tn
    halo = HALO_D
    kw = conv_w.shape[0]
    assert kw - 1 <= halo and halo % 16 == 0 and tm % CONV_ROWS == 0 and tn % V7X_LANES == 0
    hb = tm // halo
    lo = lambda j: jnp.minimum(j, nd - 1)
    hi = lambda j: jnp.maximum(j - nd, 0)
    prev = lambda j: jnp.clip(j - 1, 0, nd - 1)
    kern = functools.partial(_mixer_d_kernel, nd=nd, halo=halo, tiles_per_seq=seq // tm)
    vec = lambda a: a.reshape(nd, 1, tn)
    return pl.pallas_call(
        kern,
        grid=(t // tm, 2 * nd),
        in_specs=[pl.BlockSpec((tm, d), lambda i, j: (i, 0)),
                  pl.BlockSpec((halo, d), lambda i, j: (jnp.maximum(i * hb - 1, 0), 0)),
                  pl.BlockSpec((1, d), lambda i, j: (0, 0)),
                  pl.BlockSpec((d, tn), lambda i, j: (0, lo(j))),
                  pl.BlockSpec((d, tn), lambda i, j: (0, nd + lo(j))),
                  pl.BlockSpec((kw, tn), lambda i, j: (0, prev(j))),
                  pl.BlockSpec((1, 1, tn), lambda i, j: (prev(j), 0, 0)),
                  pl.BlockSpec((1, 1, tn), lambda i, j: (hi(j), 0, 0)),
                  pl.BlockSpec((1, 1, tn), lambda i, j: (hi(j), 0, 0)),
                  pl.BlockSpec((tn, d), lambda i, j: (hi(j), 0))],
        out_specs=pl.BlockSpec((tm, d), lambda i, j: (i, 0)),
        out_shape=jax.ShapeDtypeStruct((t, d), F32),
        scratch_shapes=[pltpu.VMEM((halo + tm, d), BF16),
                        pltpu.VMEM((nd, tm, tn), F32),
                        pltpu.VMEM((tm, 1), F32),
                        pltpu.VMEM((tm, 1), F32),
                        pltpu.VMEM((2, halo + tm, tn), F32)],
        compiler_params=_cparams(("parallel", "arbitrary")),
        name="mixer_conformer",
    )(x, x, g, w_pw1, w_pw1, conv_w, vec(conv_b), vec(ln_g), vec(ln_b), w_pw2)


def _router_kernel(x_ref, g_ref, rt_ref, e1_ref, e2_ref, g1_ref, g2_ref, r1_ref, r2_ref,
                   cnt_ref, carry_ref, *, n_experts):
    tm = x_ref.shape[0]

    @pl.when(pl.program_id(0) == 0)
    def _():
        carry_ref[...] = jnp.zeros_like(carry_ref)

    h = _rmsnorm_f32(x_ref[...], g_ref[...])
    logits = [jnp.sum(h * rt_ref[e:e + 1, :], axis=-1, keepdims=True) for e in range(n_experts)]
    m1 = logits[0]
    i1 = jnp.zeros((tm, 1), jnp.int32)
    for e in range(1, n_experts):
        better = logits[e] > m1
        m1 = jnp.where(better, logits[e], m1)
        i1 = jnp.where(better, e, i1)
    m2 = jnp.full((tm, 1), -jnp.inf, F32)
    i2 = jnp.where(i1 == 0, 1, 0).astype(jnp.int32)
    for e in range(n_experts):
        better = jnp.logical_and(i1 != e, logits[e] > m2)
        m2 = jnp.where(better, logits[e], m2)
        i2 = jnp.where(better, e, i2)
    ex = jnp.exp(m2 - m1)
    denom = 1.0 + ex
    e1_ref[...] = i1
    e2_ref[...] = i2
    g1_ref[...] = 1.0 / denom
    g2_ref[...] = ex / denom

    lane = lax.broadcasted_iota(jnp.int32, (tm, V7X_LANES), 1)
    hit1 = lane == i1
    hit2 = lane == i2
    sel = jnp.logical_or(hit1, hit2).astype(BF16)
    row = lax.broadcasted_iota(jnp.int32, (tm, tm), 0)
    col = lax.broadcasted_iota(jnp.int32, (tm, tm), 1)
    before = (col < row).astype(BF16)
    ranks = _dot(before, sel) + carry_ref[...]
    r1_ref[...] = jnp.sum(jnp.where(hit1, ranks, 0.0), axis=-1, keepdims=True).astype(jnp.int32)
    r2_ref[...] = jnp.sum(jnp.where(hit2, ranks, 0.0), axis=-1, keepdims=True).astype(jnp.int32)
    total = carry_ref[...] + jnp.sum(sel.astype(F32), axis=0, keepdims=True)
    carry_ref[...] = total
    cnt_ref[...] = jnp.broadcast_to(total, cnt_ref.shape).astype(jnp.int32)


def moe_router(x, g, router_t, *, tm):
    t, d = x.shape
    n_experts = router_t.shape[0]
    col = lambda dt: jax.ShapeDtypeStruct((t, 1), dt)
    cspec = pl.BlockSpec((tm, 1), lambda i: (i, 0))
    return pl.pallas_call(
        functools.partial(_router_kernel, n_experts=n_experts),
        grid=(t // tm,),
        in_specs=[pl.BlockSpec((tm, d), lambda i: (i, 0)),
                  pl.BlockSpec((1, d), lambda i: (0, 0)),
                  pl.BlockSpec((n_experts, d), lambda i: (0, 0))],
        out_specs=[cspec, cspec, cspec, cspec, cspec, cspec,
                   pl.BlockSpec((8, V7X_LANES), lambda i: (0, 0))],
        out_shape=[col(jnp.int32), col(jnp.int32), col(F32), col(F32),
                   col(jnp.int32), col(jnp.int32),
                   jax.ShapeDtypeStruct((8, V7X_LANES), jnp.int32)],
        scratch_shapes=[pltpu.VMEM((1, V7X_LANES), F32)],
        compiler_params=_cparams(("arbitrary",)),
        name="moe_router",
    )(x, g, router_t)


def _row_copy_wait(src_ref, dst_ref, sem, n_rows_src):
    pltpu.make_async_copy(src_ref, dst_ref.at[pl.ds(0, n_rows_src)], sem).wait()


ISSUE_UNROLL = 8


def _dispatch_kernel(p1_ref, p2_ref, zf_ref, x_ref, g_ref, xs_ref, h_ref, sem, zsem, *, tm_moe):
    i = pl.program_id(0)
    tm = x_ref.shape[0]
    n_tiles = xs_ref.shape[0] // tm_moe

    @pl.when(i == 0)
    def _():
        h_ref[...] = jnp.zeros_like(h_ref)

        def fill_copy(k, c):
            start = pl.multiple_of(k * tm_moe + c * tm, tm)
            return pltpu.make_async_copy(h_ref, xs_ref.at[pl.ds(start, tm)], zsem)

        def start_fill(k, carry):
            @pl.when(zf_ref[k] == 1)
            def _():
                for c in range(tm_moe // tm):
                    fill_copy(k, c).start()
            return carry

        def wait_fill(k, carry):
            @pl.when(zf_ref[k] == 1)
            def _():
                for c in range(tm_moe // tm):
                    fill_copy(k, c).wait()
            return carry

        lax.fori_loop(0, n_tiles, start_fill, 0)
        lax.fori_loop(0, n_tiles, wait_fill, 0)

    h_ref[...] = _rmsnorm_f32(x_ref[...], g_ref[...])

    def issue(r, carry):
        t = i * tm + r
        src = h_ref.at[pl.ds(r, 1)]
        pltpu.make_async_copy(src, xs_ref.at[pl.ds(p1_ref[t], 1)], sem).start()
        pltpu.make_async_copy(src, xs_ref.at[pl.ds(p2_ref[t], 1)], sem).start()
        return carry

    lax.fori_loop(0, tm, issue, 0, unroll=ISSUE_UNROLL)
    for _ in range(TOP_K):
        _row_copy_wait(h_ref, xs_ref, sem, tm)


def moe_dispatch(x, g, pos1, pos2, zero_fill, n_rows, *, tm, tm_moe):
    t, d = x.shape
    assert tm_moe % tm == 0
    grid_spec = pltpu.PrefetchScalarGridSpec(
        num_scalar_prefetch=3,
        grid=(t // tm,),
        in_specs=[pl.BlockSpec((tm, d), lambda i, p1, p2, zf: (i, 0)),
                  pl.BlockSpec((1, d), lambda i, p1, p2, zf: (0, 0))],
        out_specs=_ANY,
        scratch_shapes=[pltpu.VMEM((tm, d), F32), pltpu.SemaphoreType.DMA(()),
                        pltpu.SemaphoreType.DMA(())],
    )
    return pl.pallas_call(
        functools.partial(_dispatch_kernel, tm_moe=tm_moe),
        grid_spec=grid_spec,
        out_shape=jax.ShapeDtypeStruct((n_rows, d), F32),
        compiler_params=_cparams(("arbitrary",)),
        name="moe_dispatch",
    )(pos1, pos2, zero_fill, x, g)


def _moe_ffn_kernel(te_ref, xs_hbm, w1_ref, w3_ref, w2_ref, ys_ref, hn_ref, xbuf, xsem,
                    *, n_tiles):
    i = pl.program_id(0)
    n_active = te_ref[n_tiles]
    active = i < n_active

    @pl.when(pl.program_id(1) == 0)
    def _():
        ys_ref[...] = jnp.zeros_like(ys_ref)

        @pl.when(active)
        def _():
            _await_tile(xs_hbm, xbuf, xsem, i)
            hn_ref[...] = xbuf[...].astype(BF16)
            _request_tile(xs_hbm, xbuf, xsem, i + 1, n_active)

    @pl.when(active)
    def _():
        hn = hn_ref[...]
        a = _dot(hn, w1_ref[...])
        b = _dot(hn, w3_ref[...])
        act = (a * jax.nn.sigmoid(a) * b).astype(BF16)
        ys_ref[...] += _dot(act, w2_ref[...])


def moe_grouped_ffn(xs, tile_expert, w1, w3, w2, *, tm, tf):
    n_rows, d = xs.shape
    f = w1.shape[2]
    n_tiles = n_rows // tm
    nf = f // tf

    def jj(i, j, te):
        return jnp.where(i < te[n_tiles], j, nf - 1)

    grid_spec = pltpu.PrefetchScalarGridSpec(
        num_scalar_prefetch=1,
        grid=(n_tiles, nf),
        in_specs=[_ANY,
                  pl.BlockSpec((None, d, tf), lambda i, j, te: (te[i], 0, jj(i, j, te))),
                  pl.BlockSpec((None, d, tf), lambda i, j, te: (te[i], 0, jj(i, j, te))),
                  pl.BlockSpec((None, tf, d), lambda i, j, te: (te[i], jj(i, j, te), 0))],
        out_specs=pl.BlockSpec((tm, d), lambda i, j, te: (i, 0)),
        scratch_shapes=[pltpu.VMEM((tm, d), BF16)] + _XBUF(tm, d),
    )
    return pl.pallas_call(
        functools.partial(_moe_ffn_kernel, n_tiles=n_tiles),
        grid_spec=grid_spec,
        out_shape=jax.ShapeDtypeStruct((n_rows, d), F32),
        compiler_params=_cparams(("arbitrary", "arbitrary")),
        name="moe_grouped_ffn",
    )(tile_expert, xs, w1, w3, w2)


def _combine_kernel(p1_ref, p2_ref, x_ref, g1_ref, g2_ref, ys_ref, *rest, final_norm):
    if final_norm:
        fg_ref, o_ref, a_ref, b_ref, sem = rest
    else:
        o_ref, a_ref, b_ref, sem = rest
    i = pl.program_id(0)
    tm = x_ref.shape[0]

    def issue(r, carry):
        t = i * tm + r
        pltpu.make_async_copy(ys_ref.at[pl.ds(p1_ref[t], 1)], a_ref.at[pl.ds(r, 1)], sem).start()
        pltpu.make_async_copy(ys_ref.at[pl.ds(p2_ref[t], 1)], b_ref.at[pl.ds(r, 1)], sem).start()
        return carry

    lax.fori_loop(0, tm, issue, 0, unroll=ISSUE_UNROLL)
    _row_copy_wait(a_ref, ys_ref, sem, tm)
    _row_copy_wait(b_ref, ys_ref, sem, tm)
    y = x_ref[...] + (g1_ref[...] * a_ref[...] + g2_ref[...] * b_ref[...])
    if final_norm:
        y = _rmsnorm_f32(y, fg_ref[...])
    o_ref[...] = y


def moe_combine(x, g1, g2, ys, pos1, pos2, final_g, *, tm):
    t, d = x.shape
    final_norm = final_g is not None
    in_specs = [pl.BlockSpec((tm, d), lambda i, p1, p2: (i, 0)),
                pl.BlockSpec((tm, 1), lambda i, p1, p2: (i, 0)),
                pl.BlockSpec((tm, 1), lambda i, p1, p2: (i, 0)),
                pl.BlockSpec(memory_space=pl.ANY)]
    args = [pos1, pos2, x, g1, g2, ys]
    if final_norm:
        in_specs.append(pl.BlockSpec((1, d), lambda i, p1, p2: (0, 0)))
        args.append(final_g)
    grid_spec = pltpu.PrefetchScalarGridSpec(
        num_scalar_prefetch=2,
        grid=(t // tm,),
        in_specs=in_specs,
        out_specs=pl.BlockSpec((tm, d), lambda i, p1, p2: (i, 0)),
        scratch_shapes=[pltpu.VMEM((tm, d), F32), pltpu.VMEM((tm, d), F32),
                        pltpu.SemaphoreType.DMA(())],
    )
    return pl.pallas_call(
        functools.partial(_combine_kernel, final_norm=final_norm),
        grid_spec=grid_spec,
        out_shape=jax.ShapeDtypeStruct((t, d), F32),
        compiler_params=_cparams(("arbitrary",)),
        name="moe_combine",
    )(*args)


def moe_ffn(x, g, router, w1, w3, w2, final_g, *, tm_moe, tf):
    t, d = x.shape
    n_experts = router.shape[1]
    e1, e2, g1, g2, r1, r2, cnt = moe_router(x, g, router.T, tm=TM_ROW)

    counts = cnt[0, :n_experts]
    padded = ((counts + tm_moe - 1) // tm_moe) * tm_moe
    ends = jnp.cumsum(padded)
    starts = ends - padded
    pos1 = (starts[e1[:, 0]] + r1[:, 0]).astype(jnp.int32)
    pos2 = (starts[e2[:, 0]] + r2[:, 0]).astype(jnp.int32)
    n_rows = TOP_K * t + n_experts * tm_moe
    n_tiles = n_rows // tm_moe
    n_active = (ends[-1] // tm_moe).astype(jnp.int32)
    tile_row = jnp.arange(n_tiles, dtype=jnp.int32) * tm_moe
    tile_e = jnp.sum((tile_row[:, None] >= ends[None, :]).astype(jnp.int32), axis=1)
    tile_e = jnp.minimum(tile_e, n_experts - 1)
    last_e = tile_e[jnp.maximum(n_active - 1, 0)]
    tile_e = jnp.where(jnp.arange(n_tiles) < n_active, tile_e, last_e)
    tile_expert = jnp.concatenate([tile_e, n_active[None]]).astype(jnp.int32)
    group_tail = jnp.any((tile_row[:, None] + tm_moe == ends[None, :]) & (padded[None, :] > 0),
                         axis=1)
    zero_fill = (group_tail | (jnp.arange(n_tiles) >= n_active)).astype(jnp.int32)

    xs = moe_dispatch(x, g, pos1, pos2, zero_fill, n_rows, tm=TM_ROW, tm_moe=tm_moe)
    ys = moe_grouped_ffn(xs, tile_expert, w1, w3, w2, tm=tm_moe, tf=tf)
    return moe_combine(x, g1, g2, ys, pos1, pos2, final_g, tm=TM_ROW)


def kernel(x, mem, mem_norm, final_norm, mixer_norm, xattn_norm, ffn_norm, xa_wq, xa_wk, xa_wv, xa_wo, a_w_in, a_conv, a_w_out, b_w_map, b_scale, c_w_uv, c_v_norm, c_w_s, c_b_s, c_w_out, d_w_pw1, d_conv, d_conv_b, d_ln_g, d_ln_b, d_w_pw2, f_w1, f_w3, f_w2, m_router, m_w1, m_w3, m_w2):
    batch, seq, d = x.shape
    depth = mixer_norm.shape[0]
    n_mixers = 4
    row = lambda v: v.reshape(1, -1)

    def bf(w, idx):
        per_layer = 1
        for s in w.shape[1:-2]:
            per_layer *= s
        out = cast_bf16(w, idx * per_layer, per_layer)
        return out.reshape(w.shape[1:])

    mem2 = mem.reshape(batch * mem.shape[1], d)
    k_all = rms_matmul_layers(mem2, row(mem_norm), cast_bf16(xa_wk), tm=mem2.shape[0], tn=TN)
    v_all = rms_matmul_layers(mem2, row(mem_norm), cast_bf16(xa_wv), tm=mem2.shape[0], tn=TN)
    k_all = k_all.reshape(batch, mem.shape[1], depth * d)
    v_all = v_all.reshape(batch, mem.shape[1], depth * d)
    wq_all = cast_bf16(xa_wq)
    wo_all = cast_bf16(xa_wo)

    xt = x.reshape(batch * seq, d)
    for i in range(depth):
        mixer, j = i % n_mixers, i // n_mixers
        g = row(mixer_norm[i])
        if mixer == 0:
            xt = mixer_short_conv(xt, g, bf(a_w_in, j), a_conv[j], bf(a_w_out, j),
                                  seq=seq, tm=TM, tn=TN)
        elif mixer == 1:
            xt = mixer_pooling(xt, g, bf(b_w_map, j), row(b_scale[j]), seq=seq, tm=TM_VPU)
        elif mixer == 2:
            xt = mixer_gmlp(xt, g, bf(c_w_uv, j), c_v_norm[j], c_w_s[j], c_b_s[j],
                            bf(c_w_out, j), tm=TM)
        else:
            xt = mixer_conformer(xt, g, bf(d_w_pw1, j), d_conv[j], d_conv_b[j], d_ln_g[j],
                                 d_ln_b[j], bf(d_w_pw2, j), seq=seq, tm=TM_VPU, tn=TN)
        xt = cross_attention(xt, row(xattn_norm[i]), wq_all, k_all, v_all, i, wo_all,
                             seq=seq, tm=TM_XATTN)
        k = i // 2
        gf = row(ffn_norm[i])
        if i % 2 == 0:
            xt = swiglu_ffn(xt, gf, bf(f_w1, k), bf(f_w3, k), bf(f_w2, k), tm=TM, tf=TN)
        else:
            fin = row(final_norm) if i == depth - 1 else None
            xt = moe_ffn(xt, gf, m_router[k], bf(m_w1, k), bf(m_w3, k), bf(m_w2, k), fin,
                         tm_moe=TM_MOE, tf=TN)
    if depth % 2 == 1:
        raise NotImplementedError("final RMSNorm is fused into the last (expert) layer")
    return xt.reshape(batch, seq, d)
```

```python
import functools

import jax
import jax.numpy as jnp
from jax import lax
from jax.experimental import pallas as pl
from jax.experimental.pallas import tpu as pltpu

NORM_EPS = 1e-6
XATTN_HEADS = 4
POOL_WINDOWS = (2, 4, 8, 16)
GMLP_CHUNK = 128
GMLP_GROUPS = 4
TOP_K = 2

BF16 = jnp.bfloat16
F32 = jnp.float32

V7X_VMEM_BYTES = 64 * 1024 * 1024
V7X_LANES = 128
VMEM_LIMIT = 56 * 1024 * 1024

TM = 1024
TM_VPU = 512
TN = 512
TM_MOE = 1024
TM_ROW = 512
TM_XATTN = 512
CAST_ROWS = 256
CONV_ROWS = 64
HALO_A = 16
HALO_B = 16
HALO_D = 32


def _cparams(semantics):
    return pltpu.CompilerParams(dimension_semantics=semantics,
                                vmem_limit_bytes=VMEM_LIMIT)


def _rms_scale(x):
    return lax.rsqrt(jnp.mean(x * x, axis=-1, keepdims=True) + NORM_EPS)


def _rmsnorm_f32(x, g):
    return (x * _rms_scale(x)) * g


def _dot(a, b):
    return jnp.dot(a, b, preferred_element_type=F32)


def _shift_rows(z, s):
    if s == 0:
        return z
    return pltpu.roll(z, s, axis=0)


def _tile_copy(src_hbm, buf, sem, tile):
    tm = buf.shape[0]
    start = pl.multiple_of(tile * tm, tm)
    return pltpu.make_async_copy(src_hbm.at[pl.ds(start, tm)], buf, sem)


def _await_tile(src_hbm, buf, sem, i):
    @pl.when(i == 0)
    def _():
        _tile_copy(src_hbm, buf, sem, 0).start()

    _tile_copy(src_hbm, buf, sem, i).wait()


def _request_tile(src_hbm, buf, sem, nxt, n_tiles):
    @pl.when(nxt < n_tiles)
    def _():
        _tile_copy(src_hbm, buf, sem, nxt).start()


_XBUF = lambda tm, d: [pltpu.VMEM((tm, d), F32), pltpu.SemaphoreType.DMA(())]
_ANY = pl.BlockSpec(memory_space=pl.ANY)
_SEQ2 = ("arbitrary", "arbitrary")


def _halo_keep_mask(rows, halo, first_tile):
    r = lax.broadcasted_iota(jnp.int32, (rows, 1), 0)
    return jnp.logical_or(r >= halo, jnp.logical_not(first_tile))


def _cast_kernel(w_ref, o_ref):
    o_ref[...] = w_ref[...].astype(o_ref.dtype)


def cast_bf16(w, first=0, count=None):
    r, c = w.shape[-2:]
    w3 = w.reshape(-1, r, c)
    count = w3.shape[0] - first if count is None else count
    br = CAST_ROWS if r % CAST_ROWS == 0 else r
    return pl.pallas_call(
        _cast_kernel,
        grid=(count, r // br),
        in_specs=[pl.BlockSpec((None, br, c), lambda e, i: (first + e, i, 0))],
        out_specs=pl.BlockSpec((None, br, c), lambda e, i: (e, i, 0)),
        out_shape=jax.ShapeDtypeStruct((count, r, c), BF16),
        compiler_params=_cparams(("parallel", "parallel")),
        name="cast_bf16",
    )(w3)


def _rms_mm_kernel(x_ref, g_ref, w_ref, o_ref, hn_ref):
    @pl.when(pl.program_id(1) == 0)
    def _():
        hn_ref[...] = _rmsnorm_f32(x_ref[...], g_ref[...]).astype(BF16)

    o_ref[...] = _dot(hn_ref[...], w_ref[...]).astype(o_ref.dtype)


def rms_matmul_layers(x, g, w, *, tm, tn):
    m, d = x.shape
    layers, _, n = w.shape
    nb = n // tn
    return pl.pallas_call(
        _rms_mm_kernel,
        grid=(m // tm, layers * nb),
        in_specs=[pl.BlockSpec((tm, d), lambda i, j: (i, 0)),
                  pl.BlockSpec((1, d), lambda i, j: (0, 0)),
                  pl.BlockSpec((None, d, tn), lambda i, j: (j // nb, 0, j % nb))],
        out_specs=pl.BlockSpec((tm, tn), lambda i, j: (i, j)),
        out_shape=jax.ShapeDtypeStruct((m, layers * n), BF16),
        scratch_shapes=[pltpu.VMEM((tm, d), BF16)],
        compiler_params=_cparams(("parallel", "arbitrary")),
        name="mem_kv_proj",
    )(x, g, w)


def _xattn_kernel(x_ref, g_ref, wq_ref, k_ref, v_ref, wo_ref, o_ref, q_ref, oh_ref,
                  *, scale, heads):
    x = x_ref[...]
    hn = _rmsnorm_f32(x, g_ref[...]).astype(BF16)
    q_ref[...] = _dot(hn, wq_ref[...]).astype(BF16)
    dh = wq_ref.shape[1] // heads
    for h in range(heads):
        cols = slice(h * dh, (h + 1) * dh)
        s = lax.dot_general(q_ref[:, cols], k_ref[:, cols], (((1,), (1,)), ((), ())),
                            preferred_element_type=F32) * scale
        e = jnp.exp(s - jnp.max(s, axis=-1, keepdims=True))
        p = (e / jnp.sum(e, axis=-1, keepdims=True)).astype(BF16)
        oh_ref[:, cols] = _dot(p, v_ref[:, cols]).astype(BF16)
    o_ref[...] = x + _dot(oh_ref[...], wo_ref[...])


def cross_attention(x, g, wq, k_all, v_all, layer, wo, *, seq, tm):
    t, d = x.shape
    mem = k_all.shape[1]
    tiles_per_seq = seq // tm
    resident = pl.Buffered(1)
    kv_spec = pl.BlockSpec((None, mem, d), lambda i: (i // tiles_per_seq, 0, layer))
    return pl.pallas_call(
        functools.partial(_xattn_kernel, scale=float(d // XATTN_HEADS) ** -0.5,
                          heads=XATTN_HEADS),
        grid=(t // tm,),
        in_specs=[pl.BlockSpec((tm, d), lambda i: (i, 0)),
                  pl.BlockSpec((1, d), lambda i: (0, 0)),
                  pl.BlockSpec((None, d, d), lambda i: (layer, 0, 0), pipeline_mode=resident),
                  kv_spec, kv_spec,
                  pl.BlockSpec((None, d, d), lambda i: (layer, 0, 0), pipeline_mode=resident)],
        out_specs=pl.BlockSpec((tm, d), lambda i: (i, 0)),
        out_shape=jax.ShapeDtypeStruct((t, d), F32),
        scratch_shapes=[pltpu.VMEM((tm, d), BF16), pltpu.VMEM((tm, d), BF16)],
        compiler_params=_cparams(("parallel",)),
        name="cross_attention",
    )(x, g, wq, k_all, v_all, wo)


def _ffn_kernel(x_hbm, g_ref, w1_ref, w3_ref, w2_ref, o_ref, hn_ref, xbuf, xsem):
    i = pl.program_id(0)

    @pl.when(pl.program_id(1) == 0)
    def _():
        _await_tile(x_hbm, xbuf, xsem, i)
        x = xbuf[...]
        hn_ref[...] = _rmsnorm_f32(x, g_ref[...]).astype(BF16)
        o_ref[...] = x
        _request_tile(x_hbm, xbuf, xsem, i + 1, pl.num_programs(0))

    hn = hn_ref[...]
    a = _dot(hn, w1_ref[...])
    b = _dot(hn, w3_ref[...])
    act = (a * jax.nn.sigmoid(a) * b).astype(BF16)
    o_ref[...] += _dot(act, w2_ref[...])


def swiglu_ffn(x, g, w1, w3, w2, *, tm, tf):
    t, d = x.shape
    f = w1.shape[1]
    return pl.pallas_call(
        _ffn_kernel,
        grid=(t // tm, f // tf),
        in_specs=[_ANY,
                  pl.BlockSpec((1, d), lambda i, j: (0, 0)),
                  pl.BlockSpec((d, tf), lambda i, j: (0, j)),
                  pl.BlockSpec((d, tf), lambda i, j: (0, j)),
                  pl.BlockSpec((tf, d), lambda i, j: (j, 0))],
        out_specs=pl.BlockSpec((tm, d), lambda i, j: (i, 0)),
        out_shape=jax.ShapeDtypeStruct((t, d), F32),
        scratch_shapes=[pltpu.VMEM((tm, d), BF16)] + _XBUF(tm, d),
        compiler_params=_cparams(_SEQ2),
        name="swiglu_ffn",
    )(x, g, w1, w3, w2)


def _mixer_a_kernel(x_hbm, xp_ref, g_ref, wb_ref, wc_ref, wz_ref, cw_ref, wo_ref,
                    o_ref, hn_ref, xbuf, xsem, *, halo, tiles_per_seq):
    i = pl.program_id(0)
    tm = o_ref.shape[0]

    @pl.when(pl.program_id(1) == 0)
    def _():
        _await_tile(x_hbm, xbuf, xsem, i)
        x = xbuf[...]
        g = g_ref[...]
        hn_ref[pl.ds(halo, tm), :] = _rmsnorm_f32(x, g).astype(BF16)
        hn_ref[pl.ds(0, halo), :] = _rmsnorm_f32(xp_ref[...], g).astype(BF16)
        o_ref[...] = x
        _request_tile(x_hbm, xbuf, xsem, i + 1, pl.num_programs(0))

    hfull = hn_ref[...]
    bgate = _dot(hn_ref[pl.ds(halo, tm), :], wb_ref[...])
    u = _dot(hfull, wc_ref[...]) * _dot(hfull, wz_ref[...])
    keep = _halo_keep_mask(halo + tm, halo, i % tiles_per_seq == 0)
    u = jnp.where(keep, u, 0.0)
    kw = cw_ref.shape[0]
    conv = u * cw_ref[kw - 1:kw, :]
    for s in range(1, kw):
        conv = conv + _shift_rows(u, s) * cw_ref[kw - 1 - s:kw - s, :]
    y = (bgate * conv[halo:, :]).astype(BF16)
    o_ref[...] += _dot(y, wo_ref[...])


def mixer_short_conv(x, g, w_in, conv_w, w_out, *, seq, tm, tn):
    t, d = x.shape
    nd = d // tn
    halo = HALO_A
    assert conv_w.shape[0] - 1 <= halo
    hb = tm // halo
    kern = functools.partial(_mixer_a_kernel, halo=halo, tiles_per_seq=seq // tm)
    return pl.pallas_call(
        kern,
        grid=(t // tm, nd),
        in_specs=[_ANY,
                  pl.BlockSpec((halo, d), lambda i, j: (jnp.maximum(i * hb - 1, 0), 0)),
                  pl.BlockSpec((1, d), lambda i, j: (0, 0)),
                  pl.BlockSpec((d, tn), lambda i, j: (0, j)),
                  pl.BlockSpec((d, tn), lambda i, j: (0, nd + j)),
                  pl.BlockSpec((d, tn), lambda i, j: (0, 2 * nd + j)),
                  pl.BlockSpec((conv_w.shape[0], tn), lambda i, j: (0, j)),
                  pl.BlockSpec((tn, d), lambda i, j: (j, 0))],
        out_specs=pl.BlockSpec((tm, d), lambda i, j: (i, 0)),
        out_shape=jax.ShapeDtypeStruct((t, d), F32),
        scratch_shapes=[pltpu.VMEM((halo + tm, d), BF16)] + _XBUF(tm, d),
        compiler_params=_cparams(_SEQ2),
        name="mixer_short_conv",
    )(x, x, g, w_in, w_in, w_in, conv_w, w_out)


def _mixer_b_kernel(x_ref, xp_ref, g_ref, wm_ref, sc_ref, o_ref, *, halo, tiles_per_seq, windows):
    i = pl.program_id(0)
    tm, d = x_ref.shape
    dg = d // len(windows)
    x = x_ref[...]
    g = g_ref[...]
    first = i % tiles_per_seq == 0
    h_cur = _rmsnorm_f32(x, g)
    h_prev = jnp.where(first, 0.0, _rmsnorm_f32(xp_ref[...], g))
    h = jnp.concatenate([h_prev, h_cur], axis=0)
    pos = (i % tiles_per_seq) * tm + lax.broadcasted_iota(jnp.int32, (tm, 1), 0)
    for gi, w in enumerate(windows):
        hg = h[:, gi * dg:(gi + 1) * dg]
        wsum = hg
        span = 1
        while span < w:
            wsum = wsum + _shift_rows(wsum, span)
            span *= 2
        count = jnp.minimum(pos + 1, w).astype(F32)
        pooled = wsum[halo:, :] / count - hg[halo:, :]
        y = _dot(pooled.astype(BF16), wm_ref[gi]) * sc_ref[:, gi * dg:(gi + 1) * dg]
        o_ref[:, gi * dg:(gi + 1) * dg] = x[:, gi * dg:(gi + 1) * dg] + y


def mixer_pooling(x, g, w_map, scale, *, seq, tm):
    t, d = x.shape
    halo = HALO_B
    assert max(POOL_WINDOWS) <= halo and all(w & (w - 1) == 0 for w in POOL_WINDOWS)
    hb = tm // halo
    kern = functools.partial(_mixer_b_kernel, halo=halo, tiles_per_seq=seq // tm,
                             windows=POOL_WINDOWS)
    return pl.pallas_call(
        kern,
        grid=(t // tm,),
        in_specs=[pl.BlockSpec((tm, d), lambda i: (i, 0)),
                  pl.BlockSpec((halo, d), lambda i: (jnp.maximum(i * hb - 1, 0), 0)),
                  pl.BlockSpec((1, d), lambda i: (0, 0)),
                  pl.BlockSpec(w_map.shape, lambda i: (0, 0, 0)),
                  pl.BlockSpec((1, d), lambda i: (0, 0))],
        out_specs=pl.BlockSpec((tm, d), lambda i: (i, 0)),
        out_shape=jax.ShapeDtypeStruct((t, d), F32),
        compiler_params=_cparams(("parallel",)),
        name="mixer_pooling",
    )(x, x, g, w_map, scale)


def _gelu(x):
    return 0.5 * x * (1.0 + lax.erf(x * (2.0 ** -0.5)))


def _mixer_c_kernel(x_hbm, g_ref, wuv_ref, vg_ref, ws_ref, bs_ref, wo_ref,
                    o_ref, hn_ref, v_ref, ssq_ref, xbuf, xsem, *, nd, chunk):
    i = pl.program_id(0)
    j = pl.program_id(1)
    tm = o_ref.shape[0]

    @pl.when(j == 0)
    def _():
        _await_tile(x_hbm, xbuf, xsem, i)
        x = xbuf[...]
        hn_ref[...] = _rmsnorm_f32(x, g_ref[...]).astype(BF16)
        o_ref[...] = x
        ssq_ref[...] = jnp.zeros_like(ssq_ref)
        _request_tile(x_hbm, xbuf, xsem, i + 1, pl.num_programs(0))

    act = _gelu(_dot(hn_ref[...], wuv_ref[...]))

    @pl.when(j < nd)
    def _():
        v_ref[j] = act
        ssq_ref[...] += jnp.sum(act * act, axis=-1, keepdims=True)

    @pl.when(j >= nd)
    def _():
        grp = j - nd
        width = v_ref.shape[0] * v_ref.shape[2]
        r = lax.rsqrt(ssq_ref[...] * (1.0 / width) + NORM_EPS)
        vn = ((v_ref[grp] * r) * vg_ref[0]).astype(BF16)
        row = lax.broadcasted_iota(jnp.int32, (chunk, chunk), 0)
        col = lax.broadcasted_iota(jnp.int32, (chunk, chunk), 1)
        ws = jnp.where(row >= col, ws_ref[0], 0.0).astype(BF16)
        bias = bs_ref[0]
        parts = []
        for c in range(tm // chunk):
            parts.append(_dot(ws, vn[c * chunk:(c + 1) * chunk, :]) + bias)
        sv = jnp.concatenate(parts, axis=0)
        o_ref[...] += _dot((act * sv).astype(BF16), wo_ref[...])


def mixer_gmlp(x, g, w_uv, v_gain, w_s, b_s, w_out, *, tm):
    t, d = x.shape
    width = w_out.shape[0]
    tn = width // GMLP_GROUPS
    nd = GMLP_GROUPS
    chunk = GMLP_CHUNK
    assert tm % chunk == 0
    vg = v_gain.reshape(nd, 1, tn)
    bs = b_s.reshape(nd, chunk, 1)
    kern = functools.partial(_mixer_c_kernel, nd=nd, chunk=chunk)
    return pl.pallas_call(
        kern,
        grid=(t // tm, 2 * nd),
        in_specs=[_ANY,
                  pl.BlockSpec((1, d), lambda i, j: (0, 0)),
                  pl.BlockSpec((d, tn), lambda i, j: (0, jnp.where(j < nd, j + nd, j - nd))),
                  pl.BlockSpec((1, 1, tn), lambda i, j: (jnp.maximum(j - nd, 0), 0, 0)),
                  pl.BlockSpec((1, chunk, chunk), lambda i, j: (jnp.maximum(j - nd, 0), 0, 0)),
                  pl.BlockSpec((1, chunk, 1), lambda i, j: (jnp.maximum(j - nd, 0), 0, 0)),
                  pl.BlockSpec((tn, d), lambda i, j: (jnp.maximum(j - nd, 0), 0))],
        out_specs=pl.BlockSpec((tm, d), lambda i, j: (i, 0)),
        out_shape=jax.ShapeDtypeStruct((t, d), F32),
        scratch_shapes=[pltpu.VMEM((tm, d), BF16),
                        pltpu.VMEM((nd, tm, tn), F32),
                        pltpu.VMEM((tm, 1), F32)] + _XBUF(tm, d),
        compiler_params=_cparams(_SEQ2),
        name="mixer_gmlp",
    )(x, g, w_uv, vg, w_s, bs, w_out)


def _mixer_d_kernel(x_ref, xp_ref, g_ref, wa_ref, wg_ref, cw_ref, cb_ref, lg_ref, lb_ref,
                    wo_ref, o_ref, hn_ref, z_ref, mean_ref, rstd_ref, stage_ref,
                    *, nd, halo, tiles_per_seq):
    i = pl.program_id(0)
    j = pl.program_id(1)
    tm = x_ref.shape[0]
    width = z_ref.shape[0] * z_ref.shape[2]

    @pl.when(j == 0)
    def _():
        x = x_ref[...]
        g = g_ref[...]
        hn_ref[pl.ds(halo, tm), :] = _rmsnorm_f32(x, g).astype(BF16)
        hn_ref[pl.ds(0, halo), :] = _rmsnorm_f32(xp_ref[...], g).astype(BF16)
        o_ref[...] = x
        mean_ref[...] = jnp.zeros_like(mean_ref)

    def glu_into_stage(slot):
        hfull = hn_ref[...]
        z = _dot(hfull, wa_ref[...]) * jax.nn.sigmoid(_dot(hfull, wg_ref[...]))
        keep = _halo_keep_mask(halo + tm, halo, i % tiles_per_seq == 0)
        stage_ref[slot] = jnp.where(keep, z, 0.0)

    def conv_from_stage(slot, tile):
        kw = cw_ref.shape[0]
        rows = CONV_ROWS
        for r0 in range(0, tm, rows):
            rowsum = None
            for l0 in range(0, cw_ref.shape[1], V7X_LANES):
                lanes = slice(l0, l0 + V7X_LANES)
                win = stage_ref[slot, r0:r0 + halo + rows, lanes]
                acc = None
                for b in range(8):
                    wb = _shift_rows(win, b)
                    for s in range(b, kw, 8):
                        off = halo - (s - b)
                        term = wb[off:off + rows, :] * cw_ref[kw - 1 - s:kw - s, lanes]
                        acc = term if acc is None else acc + term
                acc = acc + cb_ref[0][:, lanes]
                z_ref[tile, r0:r0 + rows, lanes] = acc
                part = jnp.sum(acc, axis=-1, keepdims=True)
                rowsum = part if rowsum is None else rowsum + part
            mean_ref[r0:r0 + rows, :] += rowsum

    @pl.when(j == 0)
    def _():
        glu_into_stage(0)

    @pl.when(jnp.logical_and(j >= 1, j < nd))
    def _():
        conv_from_stage((j - 1) % 2, j - 1)
        glu_into_stage(j % 2)

    @pl.when(j == nd)
    def _():
        conv_from_stage((nd - 1) % 2, nd - 1)
        mean = mean_ref[...] * (1.0 / width)
        mean_ref[...] = mean
        var = jnp.zeros_like(mean)
        for k in range(nd):
            c = z_ref[k] - mean
            var = var + jnp.sum(c * c, axis=-1, keepdims=True)
        rstd_ref[...] = lax.rsqrt(var * (1.0 / width) + NORM_EPS)

    @pl.when(j >= nd)
    def _():
        zn = (z_ref[j - nd] - mean_ref[...]) * rstd_ref[...]
        zn = zn * lg_ref[0] + lb_ref[0]
        act = (zn * jax.nn.sigmoid(zn)).astype(BF16)
        o_ref[...] += _dot(act, wo_ref[...])


def mixer_conformer(x, g, w_pw1, conv_w, conv_b, ln_g, ln_b, w_pw2, *, seq, tm, tn):
    t, d = x.shape
    nd = d // tn
    halo = HALO_D
    kw = conv_w.shape[0]
    assert kw - 1 <= halo and halo % 16 == 0 and tm % CONV_ROWS == 0 and tn % V7X_LANES == 0
    hb = tm // halo
    lo = lambda j: jnp.minimum(j, nd - 1)
    hi = lambda j: jnp.maximum(j - nd, 0)
    prev = lambda j: jnp.clip(j - 1, 0, nd - 1)
    kern = functools.partial(_mixer_d_kernel, nd=nd, halo=halo, tiles_per_seq=seq // tm)
    vec = lambda a: a.reshape(nd, 1, tn)
    return pl.pallas_call(
        kern,
        grid=(t // tm, 2 * nd),
        in_specs=[pl.BlockSpec((tm, d), lambda i, j: (i, 0)),
                  pl.BlockSpec((halo, d), lambda i, j: (jnp.maximum(i * hb - 1, 0), 0)),
                  pl.BlockSpec((1, d), lambda i, j: (0, 0)),
                  pl.BlockSpec((d, tn), lambda i, j: (0, lo(j))),
                  pl.BlockSpec((d, tn), lambda i, j: (0, nd + lo(j))),
                  pl.BlockSpec((kw, tn), lambda i, j: (0, prev(j))),
                  pl.BlockSpec((1, 1, tn), lambda i, j: (prev(j), 0, 0)),
                  pl.BlockSpec((1, 1, tn), lambda i, j: (hi(j), 0, 0)),
                  pl.BlockSpec((1, 1, tn), lambda i, j: (hi(j), 0, 0)),
                  pl.BlockSpec((tn, d), lambda i, j: (hi(j), 0))],
        out_specs=pl.BlockSpec((tm, d), lambda i, j: (i, 0)),
        out_shape=jax.ShapeDtypeStruct((t, d), F32),
        scratch_shapes=[pltpu.VMEM((halo + tm, d), BF16),
                        pltpu.VMEM((nd, tm, tn), F32),
                        pltpu.VMEM((tm, 1), F32),
                        pltpu.VMEM((tm, 1), F32),
                        pltpu.VMEM((2, halo + tm, tn), F32)],
        compiler_params=_cparams(("parallel", "arbitrary")),
        name="mixer_conformer",
    )(x, x, g, w_pw1, w_pw1, conv_w, vec(conv_b), vec(ln_g), vec(ln_b), w_pw2)


def _router_kernel(x_ref, g_ref, rt_ref, e1_ref, e2_ref, g1_ref, g2_ref, r1_ref, r2_ref,
                   cnt_ref, carry_ref, *, n_experts):
    tm = x_ref.shape[0]

    @pl.when(pl.program_id(0) == 0)
    def _():
        carry_ref[...] = jnp.zeros_like(carry_ref)

    h = _rmsnorm_f32(x_ref[...], g_ref[...])
    logits = [jnp.sum(h * rt_ref[e:e + 1, :], axis=-1, keepdims=True) for e in range(n_experts)]
    m1 = logits[0]
    i1 = jnp.zeros((tm, 1), jnp.int32)
    for e in range(1, n_experts):
        better = logits[e] > m1
        m1 = jnp.where(better, logits[e], m1)
        i1 = jnp.where(better, e, i1)
    m2 = jnp.full((tm, 1), -jnp.inf, F32)
    i2 = jnp.where(i1 == 0, 1, 0).astype(jnp.int32)
    for e in range(n_experts):
        better = jnp.logical_and(i1 != e, logits[e] > m2)
        m2 = jnp.where(better, logits[e], m2)
        i2 = jnp.where(better, e, i2)
    ex = jnp.exp(m2 - m1)
    denom = 1.0 + ex
    e1_ref[...] = i1
    e2_ref[...] = i2
    g1_ref[...] = 1.0 / denom
    g2_ref[...] = ex / denom

    lane = lax.broadcasted_iota(jnp.int32, (tm, V7X_LANES), 1)
    hit1 = lane == i1
    hit2 = lane == i2
    sel = jnp.logical_or(hit1, hit2).astype(BF16)
    row = lax.broadcasted_iota(jnp.int32, (tm, tm), 0)
    col = lax.broadcasted_iota(jnp.int32, (tm, tm), 1)
    before = (col < row).astype(BF16)
    ranks = _dot(before, sel) + carry_ref[...]
    r1_ref[...] = jnp.sum(jnp.where(hit1, ranks, 0.0), axis=-1, keepdims=True).astype(jnp.int32)
    r2_ref[...] = jnp.sum(jnp.where(hit2, ranks, 0.0), axis=-1, keepdims=True).astype(jnp.int32)
    total = carry_ref[...] + jnp.sum(sel.astype(F32), axis=0, keepdims=True)
    carry_ref[...] = total
    cnt_ref[...] = jnp.broadcast_to(total, cnt_ref.shape).astype(jnp.int32)


def moe_router(x, g, router_t, *, tm):
    t, d = x.shape
    n_experts = router_t.shape[0]
    col = lambda dt: jax.ShapeDtypeStruct((t, 1), dt)
    cspec = pl.BlockSpec((tm, 1), lambda i: (i, 0))
    return pl.pallas_call(
        functools.partial(_router_kernel, n_experts=n_experts),
        grid=(t // tm,),
        in_specs=[pl.BlockSpec((tm, d), lambda i: (i, 0)),
                  pl.BlockSpec((1, d), lambda i: (0, 0)),
                  pl.BlockSpec((n_experts, d), lambda i: (0, 0))],
        out_specs=[cspec, cspec, cspec, cspec, cspec, cspec,
                   pl.BlockSpec((8, V7X_LANES), lambda i: (0, 0))],
        out_shape=[col(jnp.int32), col(jnp.int32), col(F32), col(F32),
                   col(jnp.int32), col(jnp.int32),
                   jax.ShapeDtypeStruct((8, V7X_LANES), jnp.int32)],
        scratch_shapes=[pltpu.VMEM((1, V7X_LANES), F32)],
        compiler_params=_cparams(("arbitrary",)),
        name="moe_router",
    )(x, g, router_t)


def _row_copy_wait(src_ref, dst_ref, sem, n_rows_src):
    pltpu.make_async_copy(src_ref, dst_ref.at[pl.ds(0, n_rows_src)], sem).wait()


ISSUE_UNROLL = 8


def _dispatch_kernel(p1_ref, p2_ref, zf_ref, x_ref, g_ref, xs_ref, h_ref, sem, zsem, *, tm_moe):
    i = pl.program_id(0)
    tm = x_ref.shape[0]
    n_tiles = xs_ref.shape[0] // tm_moe

    @pl.when(i == 0)
    def _():
        h_ref[1] = jnp.zeros(h_ref.shape[1:], F32)

        def fill_copy(k, c):
            start = pl.multiple_of(k * tm_moe + c * tm, tm)
            return pltpu.make_async_copy(h_ref.at[1], xs_ref.at[pl.ds(start, tm)], zsem)

        def start_fill(k, carry):
            @pl.when(zf_ref[k] == 1)
            def _():
                for c in range(tm_moe // tm):
                    fill_copy(k, c).start()
            return carry

        def wait_fill(k, carry):
            @pl.when(zf_ref[k] == 1)
            def _():
                for c in range(tm_moe // tm):
                    fill_copy(k, c).wait()
            return carry

        lax.fori_loop(0, n_tiles, start_fill, 0)
        lax.fori_loop(0, n_tiles, wait_fill, 0)

    slot = i % 2
    h_ref[slot] = _rmsnorm_f32(x_ref[...], g_ref[...])

    def issue(r, carry):
        t = i * tm + r
        src = h_ref.at[slot, pl.ds(r, 1)]
        pltpu.make_async_copy(src, xs_ref.at[pl.ds(p1_ref[t], 1)], sem.at[slot]).start()
        pltpu.make_async_copy(src, xs_ref.at[pl.ds(p2_ref[t], 1)], sem.at[slot]).start()
        return carry

    lax.fori_loop(0, tm, issue, 0, unroll=ISSUE_UNROLL)

    def drain(s):
        for _ in range(TOP_K):
            _row_copy_wait(h_ref.at[s], xs_ref, sem.at[s], tm)

    @pl.when(i > 0)
    def _():
        drain(1 - slot)

    @pl.when(i == pl.num_programs(0) - 1)
    def _():
        drain(slot)


def moe_dispatch(x, g, pos1, pos2, zero_fill, n_rows, *, tm, tm_moe):
    t, d = x.shape
    assert tm_moe % tm == 0
    grid_spec = pltpu.PrefetchScalarGridSpec(
        num_scalar_prefetch=3,
        grid=(t // tm,),
        in_specs=[pl.BlockSpec((tm, d), lambda i, p1, p2, zf: (i, 0)),
                  pl.BlockSpec((1, d), lambda i, p1, p2, zf: (0, 0))],
        out_specs=_ANY,
        scratch_shapes=[pltpu.VMEM((2, tm, d), F32), pltpu.SemaphoreType.DMA((2,)),
                        pltpu.SemaphoreType.DMA(())],
    )
    return pl.pallas_call(
        functools.partial(_dispatch_kernel, tm_moe=tm_moe),
        grid_spec=grid_spec,
        out_shape=jax.ShapeDtypeStruct((n_rows, d), F32),
        compiler_params=_cparams(("arbitrary",)),
        name="moe_dispatch",
    )(pos1, pos2, zero_fill, x, g)


def _moe_ffn_kernel(te_ref, xs_hbm, w1_ref, w3_ref, w2_ref, ys_ref, hn_ref, xbuf, xsem,
                    *, n_tiles):
    i = pl.program_id(0)
    n_active = te_ref[n_tiles]
    active = i < n_active

    @pl.when(pl.program_id(1) == 0)
    def _():
        ys_ref[...] = jnp.zeros_like(ys_ref)

        @pl.when(active)
        def _():
            _await_tile(xs_hbm, xbuf, xsem, i)
            hn_ref[...] = xbuf[...].astype(BF16)
            _request_tile(xs_hbm, xbuf, xsem, i + 1, n_active)

    first_half_only = te_ref[n_tiles + 1 + i] == 1

    def expert_rows(rows):
        hn = hn_ref[0:rows, :]
        a = _dot(hn, w1_ref[...])
        b = _dot(hn, w3_ref[...])
        act = (a * jax.nn.sigmoid(a) * b).astype(BF16)
        ys_ref[0:rows, :] += _dot(act, w2_ref[...])

    @pl.when(jnp.logical_and(active, jnp.logical_not(first_half_only)))
    def _():
        expert_rows(hn_ref.shape[0])

    @pl.when(jnp.logical_and(active, first_half_only))
    def _():
        expert_rows(hn_ref.shape[0] // 2)


def moe_grouped_ffn(xs, tile_expert, w1, w3, w2, *, tm, tf):
    n_rows, d = xs.shape
    f = w1.shape[2]
    n_tiles = n_rows // tm
    nf = f // tf

    def jj(i, j, te):
        return jnp.where(i < te[n_tiles], j, nf - 1)

    grid_spec = pltpu.PrefetchScalarGridSpec(
        num_scalar_prefetch=1,
        grid=(n_tiles, nf),
        in_specs=[_ANY,
                  pl.BlockSpec((None, d, tf), lambda i, j, te: (te[i], 0, jj(i, j, te))),
                  pl.BlockSpec((None, d, tf), lambda i, j, te: (te[i], 0, jj(i, j, te))),
                  pl.BlockSpec((None, tf, d), lambda i, j, te: (te[i], jj(i, j, te), 0))],
        out_specs=pl.BlockSpec((tm, d), lambda i, j, te: (i, 0)),
        scratch_shapes=[pltpu.VMEM((tm, d), BF16)] + _XBUF(tm, d),
    )
    return pl.pallas_call(
        functools.partial(_moe_ffn_kernel, n_tiles=n_tiles),
        grid_spec=grid_spec,
        out_shape=jax.ShapeDtypeStruct((n_rows, d), F32),
        compiler_params=_cparams(("arbitrary", "arbitrary")),
        name="moe_grouped_ffn",
    )(tile_expert, xs, w1, w3, w2)


def _combine_kernel(p1_ref, p2_ref, x_ref, g1_ref, g2_ref, ys_ref, *rest, final_norm):
    if final_norm:
        fg_ref, o_ref, a_ref, b_ref, sem = rest
    else:
        o_ref, a_ref, b_ref, sem = rest
    i = pl.program_id(0)
    tm = x_ref.shape[0]

    def gather_tile(tile):
        slot = tile % 2

        def issue(r, carry):
            t = tile * tm + r
            pltpu.make_async_copy(ys_ref.at[pl.ds(p1_ref[t], 1)],
                                  a_ref.at[slot, pl.ds(r, 1)], sem.at[slot]).start()
            pltpu.make_async_copy(ys_ref.at[pl.ds(p2_ref[t], 1)],
                                  b_ref.at[slot, pl.ds(r, 1)], sem.at[slot]).start()
            return carry

        lax.fori_loop(0, tm, issue, 0, unroll=ISSUE_UNROLL)

    @pl.when(i == 0)
    def _():
        gather_tile(0)

    @pl.when(i + 1 < pl.num_programs(0))
    def _():
        gather_tile(i + 1)

    slot = i % 2
    _row_copy_wait(a_ref.at[slot], ys_ref, sem.at[slot], tm)
    _row_copy_wait(b_ref.at[slot], ys_ref, sem.at[slot], tm)
    y = x_ref[...] + (g1_ref[...] * a_ref[slot] + g2_ref[...] * b_ref[slot])
    if final_norm:
        y = _rmsnorm_f32(y, fg_ref[...])
    o_ref[...] = y


def moe_combine(x, g1, g2, ys, pos1, pos2, final_g, *, tm):
    t, d = x.shape
    final_norm = final_g is not None
    in_specs = [pl.BlockSpec((tm, d), lambda i, p1, p2: (i, 0)),
                pl.BlockSpec((tm, 1), lambda i, p1, p2: (i, 0)),
                pl.BlockSpec((tm, 1), lambda i, p1, p2: (i, 0)),
                pl.BlockSpec(memory_space=pl.ANY)]
    args = [pos1, pos2, x, g1, g2, ys]
    if final_norm:
        in_specs.append(pl.BlockSpec((1, d), lambda i, p1, p2: (0, 0)))
        args.append(final_g)
    grid_spec = pltpu.PrefetchScalarGridSpec(
        num_scalar_prefetch=2,
        grid=(t // tm,),
        in_specs=in_specs,
        out_specs=pl.BlockSpec((tm, d), lambda i, p1, p2: (i, 0)),
        scratch_shapes=[pltpu.VMEM((2, tm, d), F32), pltpu.VMEM((2, tm, d), F32),
                        pltpu.SemaphoreType.DMA((2,))],
    )
    return pl.pallas_call(
        functools.partial(_combine_kernel, final_norm=final_norm),
        grid_spec=grid_spec,
        out_shape=jax.ShapeDtypeStruct((t, d), F32),
        compiler_params=_cparams(("arbitrary",)),
        name="moe_combine",
    )(*args)


def moe_ffn(x, g, router, w1, w3, w2, final_g, *, tm_moe, tf):
    t, d = x.shape
    n_experts = router.shape[1]
    e1, e2, g1, g2, r1, r2, cnt = moe_router(x, g, router.T, tm=TM_ROW)

    counts = cnt[0, :n_experts]
    padded = ((counts + tm_moe - 1) // tm_moe) * tm_moe
    ends = jnp.cumsum(padded)
    starts = ends - padded
    pos1 = (starts[e1[:, 0]] + r1[:, 0]).astype(jnp.int32)
    pos2 = (starts[e2[:, 0]] + r2[:, 0]).astype(jnp.int32)
    n_rows = TOP_K * t + n_experts * tm_moe
    n_tiles = n_rows // tm_moe
    n_active = (ends[-1] // tm_moe).astype(jnp.int32)
    tile_row = jnp.arange(n_tiles, dtype=jnp.int32) * tm_moe
    tile_e = jnp.sum((tile_row[:, None] >= ends[None, :]).astype(jnp.int32), axis=1)
    tile_e = jnp.minimum(tile_e, n_experts - 1)
    last_e = tile_e[jnp.maximum(n_active - 1, 0)]
    tile_e = jnp.where(jnp.arange(n_tiles) < n_active, tile_e, last_e)
    valid_rows = (starts + counts)[tile_e] - tile_row
    first_half_only = (valid_rows <= tm_moe // 2).astype(jnp.int32)
    tile_expert = jnp.concatenate([tile_e, n_active[None], first_half_only]).astype(jnp.int32)
    group_tail = jnp.any((tile_row[:, None] + tm_moe == ends[None, :]) & (padded[None, :] > 0),
                         axis=1)
    zero_fill = (group_tail | (jnp.arange(n_tiles) >= n_active)).astype(jnp.int32)

    xs = moe_dispatch(x, g, pos1, pos2, zero_fill, n_rows, tm=TM_ROW, tm_moe=tm_moe)
    ys = moe_grouped_ffn(xs, tile_expert, w1, w3, w2, tm=tm_moe, tf=tf)
    return moe_combine(x, g1, g2, ys, pos1, pos2, final_g, tm=TM_ROW)


def kernel(x, mem, mem_norm, final_norm, mixer_norm, xattn_norm, ffn_norm, xa_wq, xa_wk, xa_wv, xa_wo, a_w_in, a_conv, a_w_out, b_w_map, b_scale, c_w_uv, c_v_norm, c_w_s, c_b_s, c_w_out, d_w_pw1, d_conv, d_conv_b, d_ln_g, d_ln_b, d_w_pw2, f_w1, f_w3, f_w2, m_router, m_w1, m_w3, m_w2):
    batch, seq, d = x.shape
    depth = mixer_norm.shape[0]
    n_mixers = 4
    row = lambda v: v.reshape(1, -1)

    def bf(w, idx):
        per_layer = 1
        for s in w.shape[1:-2]:
            per_layer *= s
        out = cast_bf16(w, idx * per_layer, per_layer)
        return out.reshape(w.shape[1:])

    mem2 = mem.reshape(batch * mem.shape[1], d)
    k_all = rms_matmul_layers(mem2, row(mem_norm), cast_bf16(xa_wk), tm=mem2.shape[0], tn=TN)
    v_all = rms_matmul_layers(mem2, row(mem_norm), cast_bf16(xa_wv), tm=mem2.shape[0], tn=TN)
    k_all = k_all.reshape(batch, mem.shape[1], depth * d)
    v_all = v_all.reshape(batch, mem.shape[1], depth * d)
    wq_all = cast_bf16(xa_wq)
    wo_all = cast_bf16(xa_wo)

    xt = x.reshape(batch * seq, d)
    for i in range(depth):
        mixer, j = i % n_mixers, i // n_mixers
        g = row(mixer_norm[i])
        if mixer == 0:
            xt = mixer_short_conv(xt, g, bf(a_w_in, j), a_conv[j], bf(a_w_out, j),
                                  seq=seq, tm=TM, tn=TN)
        elif mixer == 1:
            xt = mixer_pooling(xt, g, bf(b_w_map, j), row(b_scale[j]), seq=seq, tm=TM_VPU)
        elif mixer == 2:
            xt = mixer_gmlp(xt, g, bf(c_w_uv, j), c_v_norm[j], c_w_s[j], c_b_s[j],
                            bf(c_w_out, j), tm=TM)
        else:
            xt = mixer_conformer(xt, g, bf(d_w_pw1, j), d_conv[j], d_conv_b[j], d_ln_g[j],
                                 d_ln_b[j], bf(d_w_pw2, j), seq=seq, tm=TM_VPU, tn=TN)
        xt = cross_attention(xt, row(xattn_norm[i]), wq_all, k_all, v_all, i, wo_all,
                             seq=seq, tm=TM_XATTN)
        k = i // 2
        gf = row(ffn_norm[i])
        if i % 2 == 0:
            xt = swiglu_ffn(xt, gf, bf(f_w1, k), bf(f_w3, k), bf(f_w2, k), tm=TM, tf=TN)
        else:
            fin = row(final_norm) if i == depth - 1 else None
            xt = moe_ffn(xt, gf, m_router[k], bf(m_w1, k), bf(m_w3, k), bf(m_w2, k), fin,
                         tm_moe=TM_MOE, tf=TN)
    if depth % 2 == 1:
        raise NotImplementedError("final RMSNorm is fused into the last (expert) layer")
    return xt.reshape(batch, seq, d)
```

```python
import functools

import jax
import jax.numpy as jnp
from jax import lax
from jax.experimental import pallas as pl
from jax.experimental.pallas import tpu as pltpu

NORM_EPS = 1e-6
XATTN_HEADS = 4
POOL_WINDOWS = (2, 4, 8, 16)
GMLP_CHUNK = 128
GMLP_GROUPS = 4
TOP_K = 2

BF16 = jnp.bfloat16
F32 = jnp.float32

V7X_VMEM_BYTES = 64 * 1024 * 1024
V7X_LANES = 128
VMEM_LIMIT = 56 * 1024 * 1024

TM = 1024
TM_VPU = 512
TN = 512
TM_MOE = 1024
TM_ROW = 512
TM_XATTN = 512
CAST_ROWS = 512
CONV_ROWS = 64
HALO_A = 16
HALO_B = 16
HALO_D = 32


def _cparams(semantics):
    return pltpu.CompilerParams(dimension_semantics=semantics,
                                vmem_limit_bytes=VMEM_LIMIT)


def _rms_scale(x):
    return lax.rsqrt(jnp.mean(x * x, axis=-1, keepdims=True) + NORM_EPS)


def _rmsnorm_f32(x, g):
    return (x * _rms_scale(x)) * g


def _dot(a, b):
    return jnp.dot(a, b, preferred_element_type=F32)


def _shift_rows(z, s):
    if s == 0:
        return z
    return pltpu.roll(z, s, axis=0)


def _tile_copy(src_hbm, buf, sem, tile):
    tm = buf.shape[0]
    start = pl.multiple_of(tile * tm, tm)
    return pltpu.make_async_copy(src_hbm.at[pl.ds(start, tm)], buf, sem)


def _await_tile(src_hbm, buf, sem, i):
    @pl.when(i == 0)
    def _():
        _tile_copy(src_hbm, buf, sem, 0).start()

    _tile_copy(src_hbm, buf, sem, i).wait()


def _request_tile(src_hbm, buf, sem, nxt, n_tiles):
    @pl.when(nxt < n_tiles)
    def _():
        _tile_copy(src_hbm, buf, sem, nxt).start()


_XBUF = lambda tm, d: [pltpu.VMEM((tm, d), F32), pltpu.SemaphoreType.DMA(())]
_ANY = pl.BlockSpec(memory_space=pl.ANY)
_SEQ2 = ("arbitrary", "arbitrary")


def _halo_keep_mask(rows, halo, first_tile):
    r = lax.broadcasted_iota(jnp.int32, (rows, 1), 0)
    return jnp.logical_or(r >= halo, jnp.logical_not(first_tile))


def _cast_kernel(w_ref, o_ref):
    o_ref[...] = w_ref[...].astype(o_ref.dtype)


def cast_bf16(w, first=0, count=None):
    r, c = w.shape[-2:]
    w3 = w.reshape(-1, r, c)
    count = w3.shape[0] - first if count is None else count
    br = CAST_ROWS if r % CAST_ROWS == 0 else r
    return pl.pallas_call(
        _cast_kernel,
        grid=(count, r // br),
        in_specs=[pl.BlockSpec((None, br, c), lambda e, i: (first + e, i, 0))],
        out_specs=pl.BlockSpec((None, br, c), lambda e, i: (e, i, 0)),
        out_shape=jax.ShapeDtypeStruct((count, r, c), BF16),
        compiler_params=_cparams(("parallel", "parallel")),
        name="cast_bf16",
    )(w3)


def _rms_mm_kernel(x_ref, g_ref, w_ref, o_ref, hn_ref):
    @pl.when(pl.program_id(1) == 0)
    def _():
        hn_ref[...] = _rmsnorm_f32(x_ref[...], g_ref[...]).astype(BF16)

    o_ref[...] = _dot(hn_ref[...], w_ref[...]).astype(o_ref.dtype)


def rms_matmul_layers(x, g, w, *, tm, tn):
    m, d = x.shape
    layers, _, n = w.shape
    nb = n // tn
    return pl.pallas_call(
        _rms_mm_kernel,
        grid=(m // tm, layers * nb),
        in_specs=[pl.BlockSpec((tm, d), lambda i, j: (i, 0)),
                  pl.BlockSpec((1, d), lambda i, j: (0, 0)),
                  pl.BlockSpec((None, d, tn), lambda i, j: (j // nb, 0, j % nb))],
        out_specs=pl.BlockSpec((tm, tn), lambda i, j: (i, j)),
        out_shape=jax.ShapeDtypeStruct((m, layers * n), BF16),
        scratch_shapes=[pltpu.VMEM((tm, d), BF16)],
        compiler_params=_cparams(("parallel", "arbitrary")),
        name="mem_kv_proj",
    )(x, g, w)


def _xattn_kernel(x_ref, g_ref, wq_ref, k_ref, v_ref, wo_ref, o_ref, q_ref, oh_ref,
                  *, scale, heads):
    x = x_ref[...]
    hn = _rmsnorm_f32(x, g_ref[...]).astype(BF16)
    q_ref[...] = _dot(hn, wq_ref[...]).astype(BF16)
    dh = wq_ref.shape[1] // heads
    for h in range(heads):
        cols = slice(h * dh, (h + 1) * dh)
        s = lax.dot_general(q_ref[:, cols], k_ref[:, cols], (((1,), (1,)), ((), ())),
                            preferred_element_type=F32) * scale
        e = jnp.exp(s - jnp.max(s, axis=-1, keepdims=True))
        p = (e / jnp.sum(e, axis=-1, keepdims=True)).astype(BF16)
        oh_ref[:, cols] = _dot(p, v_ref[:, cols]).astype(BF16)
    o_ref[...] = x + _dot(oh_ref[...], wo_ref[...])


def cross_attention(x, g, wq, k_all, v_all, layer, wo, *, seq, tm):
    t, d = x.shape
    mem = k_all.shape[1]
    tiles_per_seq = seq // tm
    resident = pl.Buffered(1)
    kv_spec = pl.BlockSpec((None, mem, d), lambda i: (i // tiles_per_seq, 0, layer))
    return pl.pallas_call(
        functools.partial(_xattn_kernel, scale=float(d // XATTN_HEADS) ** -0.5,
                          heads=XATTN_HEADS),
        grid=(t // tm,),
        in_specs=[pl.BlockSpec((tm, d), lambda i: (i, 0)),
                  pl.BlockSpec((1, d), lambda i: (0, 0)),
                  pl.BlockSpec((None, d, d), lambda i: (layer, 0, 0), pipeline_mode=resident),
                  kv_spec, kv_spec,
                  pl.BlockSpec((None, d, d), lambda i: (layer, 0, 0), pipeline_mode=resident)],
        out_specs=pl.BlockSpec((tm, d), lambda i: (i, 0)),
        out_shape=jax.ShapeDtypeStruct((t, d), F32),
        scratch_shapes=[pltpu.VMEM((tm, d), BF16), pltpu.VMEM((tm, d), BF16)],
        compiler_params=_cparams(("parallel",)),
        name="cross_attention",
    )(x, g, wq, k_all, v_all, wo)


def _ffn_kernel(x_hbm, g_ref, w1_ref, w3_ref, w2_ref, o_ref, hn_ref, xbuf, xsem):
    i = pl.program_id(0)

    @pl.when(pl.program_id(1) == 0)
    def _():
        _await_tile(x_hbm, xbuf, xsem, i)
        x = xbuf[...]
        hn_ref[...] = _rmsnorm_f32(x, g_ref[...]).astype(BF16)
        o_ref[...] = x
        _request_tile(x_hbm, xbuf, xsem, i + 1, pl.num_programs(0))

    hn = hn_ref[...]
    a = _dot(hn, w1_ref[...])
    b = _dot(hn, w3_ref[...])
    act = (a * jax.nn.sigmoid(a) * b).astype(BF16)
    o_ref[...] += _dot(act, w2_ref[...])


def swiglu_ffn(x, g, w1, w3, w2, *, tm, tf):
    t, d = x.shape
    f = w1.shape[1]
    return pl.pallas_call(
        _ffn_kernel,
        grid=(t // tm, f // tf),
        in_specs=[_ANY,
                  pl.BlockSpec((1, d), lambda i, j: (0, 0)),
                  pl.BlockSpec((d, tf), lambda i, j: (0, j)),
                  pl.BlockSpec((d, tf), lambda i, j: (0, j)),
                  pl.BlockSpec((tf, d), lambda i, j: (j, 0))],
        out_specs=pl.BlockSpec((tm, d), lambda i, j: (i, 0)),
        out_shape=jax.ShapeDtypeStruct((t, d), F32),
        scratch_shapes=[pltpu.VMEM((tm, d), BF16)] + _XBUF(tm, d),
        compiler_params=_cparams(_SEQ2),
        name="swiglu_ffn",
    )(x, g, w1, w3, w2)


def _mixer_a_kernel(x_hbm, xp_ref, g_ref, wb_ref, wc_ref, wz_ref, cw_ref, wo_ref,
                    o_ref, hn_ref, xbuf, xsem, *, halo, tiles_per_seq):
    i = pl.program_id(0)
    tm = o_ref.shape[0]

    @pl.when(pl.program_id(1) == 0)
    def _():
        _await_tile(x_hbm, xbuf, xsem, i)
        x = xbuf[...]
        g = g_ref[...]
        hn_ref[pl.ds(halo, tm), :] = _rmsnorm_f32(x, g).astype(BF16)
        hn_ref[pl.ds(0, halo), :] = _rmsnorm_f32(xp_ref[...], g).astype(BF16)
        o_ref[...] = x
        _request_tile(x_hbm, xbuf, xsem, i + 1, pl.num_programs(0))

    hfull = hn_ref[...]
    bgate = _dot(hn_ref[pl.ds(halo, tm), :], wb_ref[...])
    u = _dot(hfull, wc_ref[...]) * _dot(hfull, wz_ref[...])
    keep = _halo_keep_mask(halo + tm, halo, i % tiles_per_seq == 0)
    u = jnp.where(keep, u, 0.0)
    kw = cw_ref.shape[0]
    conv = u * cw_ref[kw - 1:kw, :]
    for s in range(1, kw):
        conv = conv + _shift_rows(u, s) * cw_ref[kw - 1 - s:kw - s, :]
    y = (bgate * conv[halo:, :]).astype(BF16)
    o_ref[...] += _dot(y, wo_ref[...])


def mixer_short_conv(x, g, w_in, conv_w, w_out, *, seq, tm, tn):
    t, d = x.shape
    nd = d // tn
    halo = HALO_A
    assert conv_w.shape[0] - 1 <= halo
    hb = tm // halo
    kern = functools.partial(_mixer_a_kernel, halo=halo, tiles_per_seq=seq // tm)
    return pl.pallas_call(
        kern,
        grid=(t // tm, nd),
        in_specs=[_ANY,
                  pl.BlockSpec((halo, d), lambda i, j: (jnp.maximum(i * hb - 1, 0), 0)),
                  pl.BlockSpec((1, d), lambda i, j: (0, 0)),
                  pl.BlockSpec((d, tn), lambda i, j: (0, j)),
                  pl.BlockSpec((d, tn), lambda i, j: (0, nd + j)),
                  pl.BlockSpec((d, tn), lambda i, j: (0, 2 * nd + j)),
                  pl.BlockSpec((conv_w.shape[0], tn), lambda i, j: (0, j)),
                  pl.BlockSpec((tn, d), lambda i, j: (j, 0))],
        out_specs=pl.BlockSpec((tm, d), lambda i, j: (i, 0)),
        out_shape=jax.ShapeDtypeStruct((t, d), F32),
        scratch_shapes=[pltpu.VMEM((halo + tm, d), BF16)] + _XBUF(tm, d),
        compiler_params=_cparams(_SEQ2),
        name="mixer_short_conv",
    )(x, x, g, w_in, w_in, w_in, conv_w, w_out)


def _mixer_b_kernel(x_ref, xp_ref, g_ref, wm_ref, sc_ref, o_ref, *, halo, tiles_per_seq, windows):
    i = pl.program_id(0)
    tm, d = x_ref.shape
    dg = d // len(windows)
    x = x_ref[...]
    g = g_ref[...]
    first = i % tiles_per_seq == 0
    h_cur = _rmsnorm_f32(x, g)
    h_prev = jnp.where(first, 0.0, _rmsnorm_f32(xp_ref[...], g))
    h = jnp.concatenate([h_prev, h_cur], axis=0)
    pos = (i % tiles_per_seq) * tm + lax.broadcasted_iota(jnp.int32, (tm, 1), 0)
    for gi, w in enumerate(windows):
        hg = h[:, gi * dg:(gi + 1) * dg]
        wsum = hg
        span = 1
        while span < w:
            wsum = wsum + _shift_rows(wsum, span)
            span *= 2
        count = jnp.minimum(pos + 1, w).astype(F32)
        pooled = wsum[halo:, :] / count - hg[halo:, :]
        y = _dot(pooled.astype(BF16), wm_ref[gi]) * sc_ref[:, gi * dg:(gi + 1) * dg]
        o_ref[:, gi * dg:(gi + 1) * dg] = x[:, gi * dg:(gi + 1) * dg] + y


def mixer_pooling(x, g, w_map, scale, *, seq, tm):
    t, d = x.shape
    halo = HALO_B
    assert max(POOL_WINDOWS) <= halo and all(w & (w - 1) == 0 for w in POOL_WINDOWS)
    hb = tm // halo
    kern = functools.partial(_mixer_b_kernel, halo=halo, tiles_per_seq=seq // tm,
                             windows=POOL_WINDOWS)
    return pl.pallas_call(
        kern,
        grid=(t // tm,),
        in_specs=[pl.BlockSpec((tm, d), lambda i: (i, 0)),
                  pl.BlockSpec((halo, d), lambda i: (jnp.maximum(i * hb - 1, 0), 0)),
                  pl.BlockSpec((1, d), lambda i: (0, 0)),
                  pl.BlockSpec(w_map.shape, lambda i: (0, 0, 0)),
                  pl.BlockSpec((1, d), lambda i: (0, 0))],
        out_specs=pl.BlockSpec((tm, d), lambda i: (i, 0)),
        out_shape=jax.ShapeDtypeStruct((t, d), F32),
        compiler_params=_cparams(("parallel",)),
        name="mixer_pooling",
    )(x, x, g, w_map, scale)


def _gelu(x):
    return 0.5 * x * (1.0 + lax.erf(x * (2.0 ** -0.5)))


def _mixer_c_kernel(x_hbm, g_ref, wuv_ref, vg_ref, ws_ref, bs_ref, wo_ref,
                    o_ref, hn_ref, v_ref, ssq_ref, xbuf, xsem, *, nd, chunk):
    i = pl.program_id(0)
    j = pl.program_id(1)
    tm = o_ref.shape[0]

    @pl.when(j == 0)
    def _():
        _await_tile(x_hbm, xbuf, xsem, i)
        x = xbuf[...]
        hn_ref[...] = _rmsnorm_f32(x, g_ref[...]).astype(BF16)
        o_ref[...] = x
        ssq_ref[...] = jnp.zeros_like(ssq_ref)
        _request_tile(x_hbm, xbuf, xsem, i + 1, pl.num_programs(0))

    act = _gelu(_dot(hn_ref[...], wuv_ref[...]))

    @pl.when(j < nd)
    def _():
        v_ref[j] = act
        ssq_ref[...] += jnp.sum(act * act, axis=-1, keepdims=True)

    @pl.when(j >= nd)
    def _():
        grp = j - nd
        width = v_ref.shape[0] * v_ref.shape[2]
        r = lax.rsqrt(ssq_ref[...] * (1.0 / width) + NORM_EPS)
        vn = ((v_ref[grp] * r) * vg_ref[0]).astype(BF16)
        row = lax.broadcasted_iota(jnp.int32, (chunk, chunk), 0)
        col = lax.broadcasted_iota(jnp.int32, (chunk, chunk), 1)
        ws = jnp.where(row >= col, ws_ref[0], 0.0).astype(BF16)
        bias = bs_ref[0]
        parts = []
        for c in range(tm // chunk):
            parts.append(_dot(ws, vn[c * chunk:(c + 1) * chunk, :]) + bias)
        sv = jnp.concatenate(parts, axis=0)
        o_ref[...] += _dot((act * sv).astype(BF16), wo_ref[...])


def mixer_gmlp(x, g, w_uv, v_gain, w_s, b_s, w_out, *, tm):
    t, d = x.shape
    width = w_out.shape[0]
    tn = width // GMLP_GROUPS
    nd = GMLP_GROUPS
    chunk = GMLP_CHUNK
    assert tm % chunk == 0
    vg = v_gain.reshape(nd, 1, tn)
    bs = b_s.reshape(nd, chunk, 1)
    kern = functools.partial(_mixer_c_kernel, nd=nd, chunk=chunk)
    return pl.pallas_call(
        kern,
        grid=(t // tm, 2 * nd),
        in_specs=[_ANY,
                  pl.BlockSpec((1, d), lambda i, j: (0, 0)),
                  pl.BlockSpec((d, tn), lambda i, j: (0, jnp.where(j < nd, j + nd, j - nd))),
                  pl.BlockSpec((1, 1, tn), lambda i, j: (jnp.maximum(j - nd, 0), 0, 0)),
                  pl.BlockSpec((1, chunk, chunk), lambda i, j: (jnp.maximum(j - nd, 0), 0, 0)),
                  pl.BlockSpec((1, chunk, 1), lambda i, j: (jnp.maximum(j - nd, 0), 0, 0)),
                  pl.BlockSpec((tn, d), lambda i, j: (jnp.maximum(j - nd, 0), 0))],
        out_specs=pl.BlockSpec((tm, d), lambda i, j: (i, 0)),
        out_shape=jax.ShapeDtypeStruct((t, d), F32),
        scratch_shapes=[pltpu.VMEM((tm, d), BF16),
                        pltpu.VMEM((nd, tm, tn), F32),
                        pltpu.VMEM((tm, 1), F32)] + _XBUF(tm, d),
        compiler_params=_cparams(_SEQ2),
        name="mixer_gmlp",
    )(x, g, w_uv, vg, w_s, bs, w_out)


def _mixer_d_kernel(x_ref, xp_ref, g_ref, wa_ref, wg_ref, cw_ref, cb_ref, lg_ref, lb_ref,
                    wo_ref, o_ref, hn_ref, z_ref, mean_ref, stage_ref,
                    *, nd, halo, tiles_per_seq):
    i = pl.program_id(0)
    j = pl.program_id(1)
    tm = x_ref.shape[0]
    width = z_ref.shape[0] * z_ref.shape[2]

    @pl.when(j == 0)
    def _():
        x = x_ref[...]
        g = g_ref[...]
        hn_ref[pl.ds(halo, tm), :] = _rmsnorm_f32(x, g).astype(BF16)
        hn_ref[pl.ds(0, halo), :] = _rmsnorm_f32(xp_ref[...], g).astype(BF16)
        mean_ref[...] = jnp.zeros_like(mean_ref)

    def glu_into_stage(slot):
        hfull = hn_ref[...]
        z = _dot(hfull, wa_ref[...]) * jax.nn.sigmoid(_dot(hfull, wg_ref[...]))
        keep = _halo_keep_mask(halo + tm, halo, i % tiles_per_seq == 0)
        stage_ref[slot] = jnp.where(keep, z, 0.0)

    def conv_from_stage(slot, tile):
        kw = cw_ref.shape[0]
        rows = CONV_ROWS
        for r0 in range(0, tm, rows):
            rowsum = None
            for l0 in range(0, cw_ref.shape[1], V7X_LANES):
                lanes = slice(l0, l0 + V7X_LANES)
                win = stage_ref[slot, r0:r0 + halo + rows, lanes]
                acc = None
                for b in range(8):
                    wb = _shift_rows(win, b)
                    for s in range(b, kw, 8):
                        off = halo - (s - b)
                        term = wb[off:off + rows, :] * cw_ref[kw - 1 - s:kw - s, lanes]
                        acc = term if acc is None else acc + term
                acc = acc + cb_ref[0][:, lanes]
                z_ref[tile, r0:r0 + rows, lanes] = acc
                part = jnp.sum(acc, axis=-1, keepdims=True)
                rowsum = part if rowsum is None else rowsum + part
            mean_ref[r0:r0 + rows, :] += rowsum

    @pl.when(j == 0)
    def _():
        glu_into_stage(0)

    @pl.when(jnp.logical_and(j >= 1, j < nd))
    def _():
        conv_from_stage((j - 1) % 2, j - 1)
        glu_into_stage(j % 2)

    @pl.when(j == nd)
    def _():
        conv_from_stage((nd - 1) % 2, nd - 1)
        mean = mean_ref[...] * (1.0 / width)
        var = jnp.zeros_like(mean)
        for k in range(nd):
            c = z_ref[k] - mean
            var = var + jnp.sum(c * c, axis=-1, keepdims=True)
        rstd = lax.rsqrt(var * (1.0 / width) + NORM_EPS)
        tn = z_ref.shape[2]
        for k in range(nd):
            zn = ((z_ref[k] - mean) * rstd) * lg_ref[k] + lb_ref[k]
            hn_ref[pl.ds(halo, tm), k * tn:(k + 1) * tn] = (zn * jax.nn.sigmoid(zn)).astype(BF16)
        o_ref[...] = x_ref[...] + _dot(hn_ref[pl.ds(halo, tm), :], wo_ref[...])


def mixer_conformer(x, g, w_pw1, conv_w, conv_b, ln_g, ln_b, w_pw2, *, seq, tm, tn):
    t, d = x.shape
    nd = d // tn
    halo = HALO_D
    kw = conv_w.shape[0]
    assert kw - 1 <= halo and halo % 16 == 0 and tm % CONV_ROWS == 0 and tn % V7X_LANES == 0
    hb = tm // halo
    lo = lambda j: jnp.minimum(j, nd - 1)
    prev = lambda j: jnp.clip(j - 1, 0, nd - 1)
    kern = functools.partial(_mixer_d_kernel, nd=nd, halo=halo, tiles_per_seq=seq // tm)
    vec = lambda a: a.reshape(nd, 1, tn)
    whole = lambda shape: pl.BlockSpec(shape, lambda i, j: (0,) * len(shape))
    return pl.pallas_call(
        kern,
        grid=(t // tm, nd + 1),
        in_specs=[pl.BlockSpec((tm, d), lambda i, j: (i, 0)),
                  pl.BlockSpec((halo, d), lambda i, j: (jnp.maximum(i * hb - 1, 0), 0)),
                  pl.BlockSpec((1, d), lambda i, j: (0, 0)),
                  pl.BlockSpec((d, tn), lambda i, j: (0, lo(j))),
                  pl.BlockSpec((d, tn), lambda i, j: (0, nd + lo(j))),
                  pl.BlockSpec((kw, tn), lambda i, j: (0, prev(j))),
                  pl.BlockSpec((1, 1, tn), lambda i, j: (prev(j), 0, 0)),
                  whole((nd, 1, tn)),
                  whole((nd, 1, tn)),
                  pl.BlockSpec((d, d), lambda i, j: (0, 0), pipeline_mode=pl.Buffered(1))],
        out_specs=pl.BlockSpec((tm, d), lambda i, j: (i, 0)),
        out_shape=jax.ShapeDtypeStruct((t, d), F32),
        scratch_shapes=[pltpu.VMEM((halo + tm, d), BF16),
                        pltpu.VMEM((nd, tm, tn), F32),
                        pltpu.VMEM((tm, 1), F32),
                        pltpu.VMEM((2, halo + tm, tn), F32)],
        compiler_params=_cparams(("parallel", "arbitrary")),
        name="mixer_conformer",
    )(x, x, g, w_pw1, w_pw1, conv_w, vec(conv_b), vec(ln_g), vec(ln_b), w_pw2)


def _router_kernel(x_ref, g_ref, rt_ref, e1_ref, e2_ref, g1_ref, g2_ref, r1_ref, r2_ref,
                   cnt_ref, carry_ref, *, n_experts):
    tm = x_ref.shape[0]

    @pl.when(pl.program_id(0) == 0)
    def _():
        carry_ref[...] = jnp.zeros_like(carry_ref)

    h = _rmsnorm_f32(x_ref[...], g_ref[...])
    logits = [jnp.sum(h * rt_ref[e:e + 1, :], axis=-1, keepdims=True) for e in range(n_experts)]
    m1 = logits[0]
    i1 = jnp.zeros((tm, 1), jnp.int32)
    for e in range(1, n_experts):
        better = logits[e] > m1
        m1 = jnp.where(better, logits[e], m1)
        i1 = jnp.where(better, e, i1)
    m2 = jnp.full((tm, 1), -jnp.inf, F32)
    i2 = jnp.where(i1 == 0, 1, 0).astype(jnp.int32)
    for e in range(n_experts):
        better = jnp.logical_and(i1 != e, logits[e] > m2)
        m2 = jnp.where(better, logits[e], m2)
        i2 = jnp.where(better, e, i2)
    ex = jnp.exp(m2 - m1)
    denom = 1.0 + ex
    e1_ref[...] = i1
    e2_ref[...] = i2
    g1_ref[...] = 1.0 / denom
    g2_ref[...] = ex / denom

    lane = lax.broadcasted_iota(jnp.int32, (tm, V7X_LANES), 1)
    hit1 = lane == i1
    hit2 = lane == i2
    sel = jnp.logical_or(hit1, hit2).astype(BF16)
    row = lax.broadcasted_iota(jnp.int32, (tm, tm), 0)
    col = lax.broadcasted_iota(jnp.int32, (tm, tm), 1)
    before = (col < row).astype(BF16)
    ranks = _dot(before, sel) + carry_ref[...]
    r1_ref[...] = jnp.sum(jnp.where(hit1, ranks, 0.0), axis=-1, keepdims=True).astype(jnp.int32)
    r2_ref[...] = jnp.sum(jnp.where(hit2, ranks, 0.0), axis=-1, keepdims=True).astype(jnp.int32)
    total = carry_ref[...] + jnp.sum(sel.astype(F32), axis=0, keepdims=True)
    carry_ref[...] = total
    cnt_ref[...] = jnp.broadcast_to(total, cnt_ref.shape).astype(jnp.int32)


def moe_router(x, g, router_t, *, tm):
    t, d = x.shape
    n_experts = router_t.shape[0]
    col = lambda dt: jax.ShapeDtypeStruct((t, 1), dt)
    cspec = pl.BlockSpec((tm, 1), lambda i: (i, 0))
    return pl.pallas_call(
        functools.partial(_router_kernel, n_experts=n_experts),
        grid=(t // tm,),
        in_specs=[pl.BlockSpec((tm, d), lambda i: (i, 0)),
                  pl.BlockSpec((1, d), lambda i: (0, 0)),
                  pl.BlockSpec((n_experts, d), lambda i: (0, 0))],
        out_specs=[cspec, cspec, cspec, cspec, cspec, cspec,
                   pl.BlockSpec((8, V7X_LANES), lambda i: (0, 0))],
        out_shape=[col(jnp.int32), col(jnp.int32), col(F32), col(F32),
                   col(jnp.int32), col(jnp.int32),
                   jax.ShapeDtypeStruct((8, V7X_LANES), jnp.int32)],
        scratch_shapes=[pltpu.VMEM((1, V7X_LANES), F32)],
        compiler_params=_cparams(("arbitrary",)),
        name="moe_router",
    )(x, g, router_t)


def _row_copy_wait(src_ref, dst_ref, sem, n_rows_src):
    pltpu.make_async_copy(src_ref, dst_ref.at[pl.ds(0, n_rows_src)], sem).wait()


ISSUE_UNROLL = 8


def _dispatch_kernel(p1_ref, p2_ref, zf_ref, x_ref, g_ref, xs_ref, h_ref, sem, zsem, *, tm_moe):
    i = pl.program_id(0)
    tm = x_ref.shape[0]
    n_tiles = xs_ref.shape[0] // tm_moe

    @pl.when(i == 0)
    def _():
        h_ref[1] = jnp.zeros(h_ref.shape[1:], F32)

        def fill_copy(k, c):
            start = pl.multiple_of(k * tm_moe + c * tm, tm)
            return pltpu.make_async_copy(h_ref.at[1], xs_ref.at[pl.ds(start, tm)], zsem)

        def start_fill(k, carry):
            @pl.when(zf_ref[k] == 1)
            def _():
                for c in range(tm_moe // tm):
                    fill_copy(k, c).start()
            return carry

        def wait_fill(k, carry):
            @pl.when(zf_ref[k] == 1)
            def _():
                for c in range(tm_moe // tm):
                    fill_copy(k, c).wait()
            return carry

        lax.fori_loop(0, n_tiles, start_fill, 0)
        lax.fori_loop(0, n_tiles, wait_fill, 0)

    slot = i % 2
    h_ref[slot] = _rmsnorm_f32(x_ref[...], g_ref[...])

    def issue(r, carry):
        t = i * tm + r
        src = h_ref.at[slot, pl.ds(r, 1)]
        pltpu.make_async_copy(src, xs_ref.at[pl.ds(p1_ref[t], 1)], sem.at[slot]).start()
        pltpu.make_async_copy(src, xs_ref.at[pl.ds(p2_ref[t], 1)], sem.at[slot]).start()
        return carry

    lax.fori_loop(0, tm, issue, 0, unroll=ISSUE_UNROLL)

    def drain(s):
        for _ in range(TOP_K):
            _row_copy_wait(h_ref.at[s], xs_ref, sem.at[s], tm)

    @pl.when(i > 0)
    def _():
        drain(1 - slot)

    @pl.when(i == pl.num_programs(0) - 1)
    def _():
        drain(slot)


def moe_dispatch(x, g, pos1, pos2, zero_fill, n_rows, *, tm, tm_moe):
    t, d = x.shape
    assert tm_moe % tm == 0
    grid_spec = pltpu.PrefetchScalarGridSpec(
        num_scalar_prefetch=3,
        grid=(t // tm,),
        in_specs=[pl.BlockSpec((tm, d), lambda i, p1, p2, zf: (i, 0)),
                  pl.BlockSpec((1, d), lambda i, p1, p2, zf: (0, 0))],
        out_specs=_ANY,
        scratch_shapes=[pltpu.VMEM((2, tm, d), F32), pltpu.SemaphoreType.DMA((2,)),
                        pltpu.SemaphoreType.DMA(())],
    )
    return pl.pallas_call(
        functools.partial(_dispatch_kernel, tm_moe=tm_moe),
        grid_spec=grid_spec,
        out_shape=jax.ShapeDtypeStruct((n_rows, d), F32),
        compiler_params=_cparams(("arbitrary",)),
        name="moe_dispatch",
    )(pos1, pos2, zero_fill, x, g)


def _moe_ffn_kernel(te_ref, xs_hbm, w1_ref, w3_ref, w2_ref, ys_ref, hn_ref, xbuf, xsem,
                    *, n_tiles):
    i = pl.program_id(0)
    n_active = te_ref[n_tiles]
    active = i < n_active

    @pl.when(pl.program_id(1) == 0)
    def _():
        ys_ref[...] = jnp.zeros_like(ys_ref)

        @pl.when(active)
        def _():
            _await_tile(xs_hbm, xbuf, xsem, i)
            hn_ref[...] = xbuf[...].astype(BF16)
            _request_tile(xs_hbm, xbuf, xsem, i + 1, n_active)

    first_half_only = te_ref[n_tiles + 1 + i] == 1

    def expert_rows(rows):
        hn = hn_ref[0:rows, :]
        a = _dot(hn, w1_ref[...])
        b = _dot(hn, w3_ref[...])
        act = (a * jax.nn.sigmoid(a) * b).astype(BF16)
        ys_ref[0:rows, :] += _dot(act, w2_ref[...])

    @pl.when(jnp.logical_and(active, jnp.logical_not(first_half_only)))
    def _():
        expert_rows(hn_ref.shape[0])

    @pl.when(jnp.logical_and(active, first_half_only))
    def _():
        expert_rows(hn_ref.shape[0] // 2)


def moe_grouped_ffn(xs, tile_expert, w1, w3, w2, *, tm, tf):
    n_rows, d = xs.shape
    f = w1.shape[2]
    n_tiles = n_rows // tm
    nf = f // tf

    def jj(i, j, te):
        return jnp.where(i < te[n_tiles], j, nf - 1)

    grid_spec = pltpu.PrefetchScalarGridSpec(
        num_scalar_prefetch=1,
        grid=(n_tiles, nf),
        in_specs=[_ANY,
                  pl.BlockSpec((None, d, tf), lambda i, j, te: (te[i], 0, jj(i, j, te))),
                  pl.BlockSpec((None, d, tf), lambda i, j, te: (te[i], 0, jj(i, j, te))),
                  pl.BlockSpec((None, tf, d), lambda i, j, te: (te[i], jj(i, j, te), 0))],
        out_specs=pl.BlockSpec((tm, d), lambda i, j, te: (i, 0)),
        scratch_shapes=[pltpu.VMEM((tm, d), BF16)] + _XBUF(tm, d),
    )
    return pl.pallas_call(
        functools.partial(_moe_ffn_kernel, n_tiles=n_tiles),
        grid_spec=grid_spec,
        out_shape=jax.ShapeDtypeStruct((n_rows, d), F32),
        compiler_params=_cparams(("arbitrary", "arbitrary")),
        name="moe_grouped_ffn",
    )(tile_expert, xs, w1, w3, w2)


def _combine_kernel(p1_ref, p2_ref, x_ref, g1_ref, g2_ref, ys_ref, *rest, final_norm):
    if final_norm:
        fg_ref, o_ref, a_ref, b_ref, sem = rest
    else:
        o_ref, a_ref, b_ref, sem = rest
    i = pl.program_id(0)
    tm = x_ref.shape[0]

    def gather_tile(tile):
        slot = tile % 2

        def issue(r, carry):
            t = tile * tm + r
            pltpu.make_async_copy(ys_ref.at[pl.ds(p1_ref[t], 1)],
                                  a_ref.at[slot, pl.ds(r, 1)], sem.at[slot]).start()
            pltpu.make_async_copy(ys_ref.at[pl.ds(p2_ref[t], 1)],
                                  b_ref.at[slot, pl.ds(r, 1)], sem.at[slot]).start()
            return carry

        lax.fori_loop(0, tm, issue, 0, unroll=ISSUE_UNROLL)

    @pl.when(i == 0)
    def _():
        gather_tile(0)

    @pl.when(i + 1 < pl.num_programs(0))
    def _():
        gather_tile(i + 1)

    slot = i % 2
    _row_copy_wait(a_ref.at[slot], ys_ref, sem.at[slot], tm)
    _row_copy_wait(b_ref.at[slot], ys_ref, sem.at[slot], tm)
    y = x_ref[...] + (g1_ref[...] * a_ref[slot] + g2_ref[...] * b_ref[slot])
    if final_norm:
        y = _rmsnorm_f32(y, fg_ref[...])
    o_ref[...] = y


def moe_combine(x, g1, g2, ys, pos1, pos2, final_g, *, tm):
    t, d = x.shape
    final_norm = final_g is not None
    in_specs = [pl.BlockSpec((tm, d), lambda i, p1, p2: (i, 0)),
                pl.BlockSpec((tm, 1), lambda i, p1, p2: (i, 0)),
                pl.BlockSpec((tm, 1), lambda i, p1, p2: (i, 0)),
                pl.BlockSpec(memory_space=pl.ANY)]
    args = [pos1, pos2, x, g1, g2, ys]
    if final_norm:
        in_specs.append(pl.BlockSpec((1, d), lambda i, p1, p2: (0, 0)))
        args.append(final_g)
    grid_spec = pltpu.PrefetchScalarGridSpec(
        num_scalar_prefetch=2,
        grid=(t // tm,),
        in_specs=in_specs,
        out_specs=pl.BlockSpec((tm, d), lambda i, p1, p2: (i, 0)),
        scratch_shapes=[pltpu.VMEM((2, tm, d), F32), pltpu.VMEM((2, tm, d), F32),
                        pltpu.SemaphoreType.DMA((2,))],
    )
    return pl.pallas_call(
        functools.partial(_combine_kernel, final_norm=final_norm),
        grid_spec=grid_spec,
        out_shape=jax.ShapeDtypeStruct((t, d), F32),
        compiler_params=_cparams(("arbitrary",)),
        name="moe_combine",
    )(*args)


def moe_ffn(x, g, router, w1, w3, w2, final_g, *, tm_moe, tf):
    t, d = x.shape
    n_experts = router.shape[1]
    e1, e2, g1, g2, r1, r2, cnt = moe_router(x, g, router.T, tm=TM_ROW)

    counts = cnt[0, :n_experts]
    padded = ((counts + tm_moe - 1) // tm_moe) * tm_moe
    ends = jnp.cumsum(padded)
    starts = ends - padded
    pos1 = (starts[e1[:, 0]] + r1[:, 0]).astype(jnp.int32)
    pos2 = (starts[e2[:, 0]] + r2[:, 0]).astype(jnp.int32)
    n_rows = TOP_K * t + n_experts * tm_moe
    n_tiles = n_rows // tm_moe
    n_active = (ends[-1] // tm_moe).astype(jnp.int32)
    tile_row = jnp.arange(n_tiles, dtype=jnp.int32) * tm_moe
    tile_e = jnp.sum((tile_row[:, None] >= ends[None, :]).astype(jnp.int32), axis=1)
    tile_e = jnp.minimum(tile_e, n_experts - 1)
    last_e = tile_e[jnp.maximum(n_active - 1, 0)]
    tile_e = jnp.where(jnp.arange(n_tiles) < n_active, tile_e, last_e)
    valid_rows = (starts + counts)[tile_e] - tile_row
    first_half_only = (valid_rows <= tm_moe // 2).astype(jnp.int32)
    tile_expert = jnp.concatenate([tile_e, n_active[None], first_half_only]).astype(jnp.int32)
    group_tail = jnp.any((tile_row[:, None] + tm_moe == ends[None, :]) & (padded[None, :] > 0),
                         axis=1)
    zero_fill = (group_tail | (jnp.arange(n_tiles) >= n_active)).astype(jnp.int32)

    xs = moe_dispatch(x, g, pos1, pos2, zero_fill, n_rows, tm=TM_ROW, tm_moe=tm_moe)
    ys = moe_grouped_ffn(xs, tile_expert, w1, w3, w2, tm=tm_moe, tf=tf)
    return moe_combine(x, g1, g2, ys, pos1, pos2, final_g, tm=TM_ROW)


def kernel(x, mem, mem_norm, final_norm, mixer_norm, xattn_norm, ffn_norm, xa_wq, xa_wk, xa_wv, xa_wo, a_w_in, a_conv, a_w_out, b_w_map, b_scale, c_w_uv, c_v_norm, c_w_s, c_b_s, c_w_out, d_w_pw1, d_conv, d_conv_b, d_ln_g, d_ln_b, d_w_pw2, f_w1, f_w3, f_w2, m_router, m_w1, m_w3, m_w2):
    batch, seq, d = x.shape
    depth = mixer_norm.shape[0]
    n_mixers = 4
    row = lambda v: v.reshape(1, -1)

    def bf(w, idx):
        per_layer = 1
        for s in w.shape[1:-2]:
            per_layer *= s
        out = cast_bf16(w, idx * per_layer, per_layer)
        return out.reshape(w.shape[1:])

    mem2 = mem.reshape(batch * mem.shape[1], d)
    k_all = rms_matmul_layers(mem2, row(mem_norm), cast_bf16(xa_wk), tm=mem2.shape[0], tn=TN)
    v_all = rms_matmul_layers(mem2, row(mem_norm), cast_bf16(xa_wv), tm=mem2.shape[0], tn=TN)
    k_all = k_all.reshape(batch, mem.shape[1], depth * d)
    v_all = v_all.reshape(batch, mem.shape[1], depth * d)
    wq_all = cast_bf16(xa_wq)
    wo_all = cast_bf16(xa_wo)

    xt = x.reshape(batch * seq, d)
    for i in range(depth):
        mixer, j = i % n_mixers, i // n_mixers
        g = row(mixer_norm[i])
        if mixer == 0:
            xt = mixer_short_conv(xt, g, bf(a_w_in, j), a_conv[j], bf(a_w_out, j),
                                  seq=seq, tm=TM, tn=TN)
        elif mixer == 1:
            xt = mixer_pooling(xt, g, bf(b_w_map, j), row(b_scale[j]), seq=seq, tm=TM_VPU)
        elif mixer == 2:
            xt = mixer_gmlp(xt, g, bf(c_w_uv, j), c_v_norm[j], c_w_s[j], c_b_s[j],
                            bf(c_w_out, j), tm=TM)
        else:
            xt = mixer_conformer(xt, g, bf(d_w_pw1, j), d_conv[j], d_conv_b[j], d_ln_g[j],
                                 d_ln_b[j], bf(d_w_pw2, j), seq=seq, tm=TM_VPU, tn=TN)
        xt = cross_attention(xt, row(xattn_norm[i]), wq_all, k_all, v_all, i, wo_all,
                             seq=seq, tm=TM_XATTN)
        k = i // 2
        gf = row(ffn_norm[i])
        if i % 2 == 0:
            xt = swiglu_ffn(xt, gf, bf(f_w1, k), bf(f_w3, k), bf(f_w2, k), tm=TM, tf=TN)
        else:
            fin = row(final_norm) if i == depth - 1 else None
            xt = moe_ffn(xt, gf, m_router[k], bf(m_w1, k), bf(m_w3, k), bf(m_w2, k), fin,
                         tm_moe=TM_MOE, tf=TN)
    if depth % 2 == 1:
        raise NotImplementedError("final RMSNorm is fused into the last (expert) layer")
    return xt.reshape(batch, seq, d)
```

```python
import functools

import jax
import jax.numpy as jnp
from jax import lax
from jax.experimental import pallas as pl
from jax.experimental.pallas import tpu as pltpu

NORM_EPS = 1e-6
XATTN_HEADS = 4
POOL_WINDOWS = (2, 4, 8, 16)
GMLP_CHUNK = 128
GMLP_GROUPS = 4
TOP_K = 2

BF16 = jnp.bfloat16
F32 = jnp.float32

V7X_VMEM_BYTES = 64 * 1024 * 1024
V7X_LANES = 128
VMEM_LIMIT = 56 * 1024 * 1024

TM = 1024
TM_VPU = 512
TN = 512
TM_MOE = 1024
TM_ROW = 512
TM_XATTN = 512
CAST_ROWS = 512
TAIL_PARTS = 4
CONV_ROWS = 64
HALO_A = 16
HALO_B = 16
HALO_D = 32


def _cparams(semantics):
    return pltpu.CompilerParams(dimension_semantics=semantics,
                                vmem_limit_bytes=VMEM_LIMIT)


def _rms_scale(x):
    return lax.rsqrt(jnp.mean(x * x, axis=-1, keepdims=True) + NORM_EPS)


def _rmsnorm_f32(x, g):
    return (x * _rms_scale(x)) * g


def _dot(a, b):
    return jnp.dot(a, b, preferred_element_type=F32)


def _shift_rows(z, s):
    if s == 0:
        return z
    return pltpu.roll(z, s, axis=0)


def _tile_copy(src_hbm, buf, sem, tile):
    tm = buf.shape[0]
    start = pl.multiple_of(tile * tm, tm)
    return pltpu.make_async_copy(src_hbm.at[pl.ds(start, tm)], buf, sem)


def _await_tile(src_hbm, buf, sem, i):
    @pl.when(i == 0)
    def _():
        _tile_copy(src_hbm, buf, sem, 0).start()

    _tile_copy(src_hbm, buf, sem, i).wait()


def _request_tile(src_hbm, buf, sem, nxt, n_tiles):
    @pl.when(nxt < n_tiles)
    def _():
        _tile_copy(src_hbm, buf, sem, nxt).start()


_XBUF = lambda tm, d: [pltpu.VMEM((tm, d), F32), pltpu.SemaphoreType.DMA(())]
_ANY = pl.BlockSpec(memory_space=pl.ANY)
_SEQ2 = ("arbitrary", "arbitrary")


def _halo_keep_mask(rows, halo, first_tile):
    r = lax.broadcasted_iota(jnp.int32, (rows, 1), 0)
    return jnp.logical_or(r >= halo, jnp.logical_not(first_tile))


def _cast_kernel(w_ref, o_ref):
    o_ref[...] = w_ref[...].astype(o_ref.dtype)


def cast_bf16(w, first=0, count=None):
    r, c = w.shape[-2:]
    w3 = w.reshape(-1, r, c)
    count = w3.shape[0] - first if count is None else count
    br = CAST_ROWS if r % CAST_ROWS == 0 else r
    return pl.pallas_call(
        _cast_kernel,
        grid=(count, r // br),
        in_specs=[pl.BlockSpec((None, br, c), lambda e, i: (first + e, i, 0))],
        out_specs=pl.BlockSpec((None, br, c), lambda e, i: (e, i, 0)),
        out_shape=jax.ShapeDtypeStruct((count, r, c), BF16),
        compiler_params=_cparams(("parallel", "parallel")),
        name="cast_bf16",
    )(w3)


def _rms_mm_kernel(x_ref, g_ref, w_ref, o_ref, hn_ref):
    @pl.when(pl.program_id(1) == 0)
    def _():
        hn_ref[...] = _rmsnorm_f32(x_ref[...], g_ref[...]).astype(BF16)

    o_ref[...] = _dot(hn_ref[...], w_ref[...]).astype(o_ref.dtype)


def rms_matmul_layers(x, g, w, *, tm, tn):
    m, d = x.shape
    layers, _, n = w.shape
    nb = n // tn
    return pl.pallas_call(
        _rms_mm_kernel,
        grid=(m // tm, layers * nb),
        in_specs=[pl.BlockSpec((tm, d), lambda i, j: (i, 0)),
                  pl.BlockSpec((1, d), lambda i, j: (0, 0)),
                  pl.BlockSpec((None, d, tn), lambda i, j: (j // nb, 0, j % nb))],
        out_specs=pl.BlockSpec((tm, tn), lambda i, j: (i, j)),
        out_shape=jax.ShapeDtypeStruct((m, layers * n), BF16),
        scratch_shapes=[pltpu.VMEM((tm, d), BF16)],
        compiler_params=_cparams(("parallel", "arbitrary")),
        name="mem_kv_proj",
    )(x, g, w)


def _xattn_kernel(x_ref, g_ref, wq_ref, k_ref, v_ref, wo_ref, o_ref, q_ref, oh_ref,
                  *, scale, heads):
    x = x_ref[...]
    hn = _rmsnorm_f32(x, g_ref[...]).astype(BF16)
    q_ref[...] = _dot(hn, wq_ref[...]).astype(BF16)
    dh = wq_ref.shape[1] // heads
    for h in range(heads):
        cols = slice(h * dh, (h + 1) * dh)
        s = lax.dot_general(q_ref[:, cols], k_ref[:, cols], (((1,), (1,)), ((), ())),
                            preferred_element_type=F32) * scale
        e = jnp.exp(s - jnp.max(s, axis=-1, keepdims=True))
        p = (e / jnp.sum(e, axis=-1, keepdims=True)).astype(BF16)
        oh_ref[:, cols] = _dot(p, v_ref[:, cols]).astype(BF16)
    o_ref[...] = x + _dot(oh_ref[...], wo_ref[...])


def cross_attention(x, g, wq, k_all, v_all, layer, wo, *, seq, tm):
    t, d = x.shape
    mem = k_all.shape[1]
    tiles_per_seq = seq // tm
    resident = pl.Buffered(1)
    kv_spec = pl.BlockSpec((None, mem, d), lambda i: (i // tiles_per_seq, 0, layer))
    return pl.pallas_call(
        functools.partial(_xattn_kernel, scale=float(d // XATTN_HEADS) ** -0.5,
                          heads=XATTN_HEADS),
        grid=(t // tm,),
        in_specs=[pl.BlockSpec((tm, d), lambda i: (i, 0)),
                  pl.BlockSpec((1, d), lambda i: (0, 0)),
                  pl.BlockSpec((None, d, d), lambda i: (layer, 0, 0), pipeline_mode=resident),
                  kv_spec, kv_spec,
                  pl.BlockSpec((None, d, d), lambda i: (layer, 0, 0), pipeline_mode=resident)],
        out_specs=pl.BlockSpec((tm, d), lambda i: (i, 0)),
        out_shape=jax.ShapeDtypeStruct((t, d), F32),
        scratch_shapes=[pltpu.VMEM((tm, d), BF16), pltpu.VMEM((tm, d), BF16)],
        compiler_params=_cparams(("parallel",)),
        name="cross_attention",
    )(x, g, wq, k_all, v_all, wo)


def _ffn_kernel(x_hbm, g_ref, w1_ref, w3_ref, w2_ref, o_ref, hn_ref, xbuf, xsem):
    i = pl.program_id(0)

    @pl.when(pl.program_id(1) == 0)
    def _():
        _await_tile(x_hbm, xbuf, xsem, i)
        x = xbuf[...]
        hn_ref[...] = _rmsnorm_f32(x, g_ref[...]).astype(BF16)
        o_ref[...] = x
        _request_tile(x_hbm, xbuf, xsem, i + 1, pl.num_programs(0))

    hn = hn_ref[...]
    a = _dot(hn, w1_ref[...])
    b = _dot(hn, w3_ref[...])
    act = (a * jax.nn.sigmoid(a) * b).astype(BF16)
    o_ref[...] += _dot(act, w2_ref[...])


def swiglu_ffn(x, g, w1, w3, w2, *, tm, tf):
    t, d = x.shape
    f = w1.shape[1]
    return pl.pallas_call(
        _ffn_kernel,
        grid=(t // tm, f // tf),
        in_specs=[_ANY,
                  pl.BlockSpec((1, d), lambda i, j: (0, 0)),
                  pl.BlockSpec((d, tf), lambda i, j: (0, j)),
                  pl.BlockSpec((d, tf), lambda i, j: (0, j)),
                  pl.BlockSpec((tf, d), lambda i, j: (j, 0))],
        out_specs=pl.BlockSpec((tm, d), lambda i, j: (i, 0)),
        out_shape=jax.ShapeDtypeStruct((t, d), F32),
        scratch_shapes=[pltpu.VMEM((tm, d), BF16)] + _XBUF(tm, d),
        compiler_params=_cparams(_SEQ2),
        name="swiglu_ffn",
    )(x, g, w1, w3, w2)


def _mixer_a_kernel(x_hbm, xp_ref, g_ref, wb_ref, wc_ref, wz_ref, cw_ref, wo_ref,
                    o_ref, hn_ref, xbuf, xsem, *, halo, tiles_per_seq):
    i = pl.program_id(0)
    tm = o_ref.shape[0]

    @pl.when(pl.program_id(1) == 0)
    def _():
        _await_tile(x_hbm, xbuf, xsem, i)
        x = xbuf[...]
        g = g_ref[...]
        hn_ref[pl.ds(halo, tm), :] = _rmsnorm_f32(x, g).astype(BF16)
        hn_ref[pl.ds(0, halo), :] = _rmsnorm_f32(xp_ref[...], g).astype(BF16)
        o_ref[...] = x
        _request_tile(x_hbm, xbuf, xsem, i + 1, pl.num_programs(0))

    hfull = hn_ref[...]
    bgate = _dot(hn_ref[pl.ds(halo, tm), :], wb_ref[...])
    u = _dot(hfull, wc_ref[...]) * _dot(hfull, wz_ref[...])
    keep = _halo_keep_mask(halo + tm, halo, i % tiles_per_seq == 0)
    u = jnp.where(keep, u, 0.0)
    kw = cw_ref.shape[0]
    conv = u * cw_ref[kw - 1:kw, :]
    for s in range(1, kw):
        conv = conv + _shift_rows(u, s) * cw_ref[kw - 1 - s:kw - s, :]
    y = (bgate * conv[halo:, :]).astype(BF16)
    o_ref[...] += _dot(y, wo_ref[...])


def mixer_short_conv(x, g, w_in, conv_w, w_out, *, seq, tm, tn):
    t, d = x.shape
    nd = d // tn
    halo = HALO_A
    assert conv_w.shape[0] - 1 <= halo
    hb = tm // halo
    kern = functools.partial(_mixer_a_kernel, halo=halo, tiles_per_seq=seq // tm)
    return pl.pallas_call(
        kern,
        grid=(t // tm, nd),
        in_specs=[_ANY,
                  pl.BlockSpec((halo, d), lambda i, j: (jnp.maximum(i * hb - 1, 0), 0)),
                  pl.BlockSpec((1, d), lambda i, j: (0, 0)),
                  pl.BlockSpec((d, tn), lambda i, j: (0, j)),
                  pl.BlockSpec((d, tn), lambda i, j: (0, nd + j)),
                  pl.BlockSpec((d, tn), lambda i, j: (0, 2 * nd + j)),
                  pl.BlockSpec((conv_w.shape[0], tn), lambda i, j: (0, j)),
                  pl.BlockSpec((tn, d), lambda i, j: (j, 0))],
        out_specs=pl.BlockSpec((tm, d), lambda i, j: (i, 0)),
        out_shape=jax.ShapeDtypeStruct((t, d), F32),
        scratch_shapes=[pltpu.VMEM((halo + tm, d), BF16)] + _XBUF(tm, d),
        compiler_params=_cparams(_SEQ2),
        name="mixer_short_conv",
    )(x, x, g, w_in, w_in, w_in, conv_w, w_out)


def _mixer_b_kernel(x_ref, xp_ref, g_ref, wm_ref, sc_ref, o_ref, *, halo, tiles_per_seq, windows):
    i = pl.program_id(0)
    tm, d = x_ref.shape
    dg = d // len(windows)
    x = x_ref[...]
    g = g_ref[...]
    first = i % tiles_per_seq == 0
    h_cur = _rmsnorm_f32(x, g)
    h_prev = jnp.where(first, 0.0, _rmsnorm_f32(xp_ref[...], g))
    h = jnp.concatenate([h_prev, h_cur], axis=0)
    pos = (i % tiles_per_seq) * tm + lax.broadcasted_iota(jnp.int32, (tm, 1), 0)
    for gi, w in enumerate(windows):
        hg = h[:, gi * dg:(gi + 1) * dg]
        wsum = hg
        span = 1
        while span < w:
            wsum = wsum + _shift_rows(wsum, span)
            span *= 2
        count = jnp.minimum(pos + 1, w).astype(F32)
        pooled = wsum[halo:, :] / count - hg[halo:, :]
        y = _dot(pooled.astype(BF16), wm_ref[gi]) * sc_ref[:, gi * dg:(gi + 1) * dg]
        o_ref[:, gi * dg:(gi + 1) * dg] = x[:, gi * dg:(gi + 1) * dg] + y


def mixer_pooling(x, g, w_map, scale, *, seq, tm):
    t, d = x.shape
    halo = HALO_B
    assert max(POOL_WINDOWS) <= halo and all(w & (w - 1) == 0 for w in POOL_WINDOWS)
    hb = tm // halo
    kern = functools.partial(_mixer_b_kernel, halo=halo, tiles_per_seq=seq // tm,
                             windows=POOL_WINDOWS)
    return pl.pallas_call(
        kern,
        grid=(t // tm,),
        in_specs=[pl.BlockSpec((tm, d), lambda i: (i, 0)),
                  pl.BlockSpec((halo, d), lambda i: (jnp.maximum(i * hb - 1, 0), 0)),
                  pl.BlockSpec((1, d), lambda i: (0, 0)),
                  pl.BlockSpec(w_map.shape, lambda i: (0, 0, 0)),
                  pl.BlockSpec((1, d), lambda i: (0, 0))],
        out_specs=pl.BlockSpec((tm, d), lambda i: (i, 0)),
        out_shape=jax.ShapeDtypeStruct((t, d), F32),
        compiler_params=_cparams(("parallel",)),
        name="mixer_pooling",
    )(x, x, g, w_map, scale)


def _gelu(x):
    return 0.5 * x * (1.0 + lax.erf(x * (2.0 ** -0.5)))


def _mixer_c_kernel(x_hbm, g_ref, wuv_ref, vg_ref, ws_ref, bs_ref, wo_ref,
                    o_ref, hn_ref, v_ref, ssq_ref, xbuf, xsem, *, nd, chunk):
    i = pl.program_id(0)
    j = pl.program_id(1)
    tm = o_ref.shape[0]

    @pl.when(j == 0)
    def _():
        _await_tile(x_hbm, xbuf, xsem, i)
        x = xbuf[...]
        hn_ref[...] = _rmsnorm_f32(x, g_ref[...]).astype(BF16)
        o_ref[...] = x
        ssq_ref[...] = jnp.zeros_like(ssq_ref)
        _request_tile(x_hbm, xbuf, xsem, i + 1, pl.num_programs(0))

    act = _gelu(_dot(hn_ref[...], wuv_ref[...]))

    @pl.when(j < nd)
    def _():
        v_ref[j] = act
        ssq_ref[...] += jnp.sum(act * act, axis=-1, keepdims=True)

    @pl.when(j >= nd)
    def _():
        grp = j - nd
        width = v_ref.shape[0] * v_ref.shape[2]
        r = lax.rsqrt(ssq_ref[...] * (1.0 / width) + NORM_EPS)
        vn = ((v_ref[grp] * r) * vg_ref[0]).astype(BF16)
        row = lax.broadcasted_iota(jnp.int32, (chunk, chunk), 0)
        col = lax.broadcasted_iota(jnp.int32, (chunk, chunk), 1)
        ws = jnp.where(row >= col, ws_ref[0], 0.0).astype(BF16)
        bias = bs_ref[0]
        parts = []
        for c in range(tm // chunk):
            parts.append(_dot(ws, vn[c * chunk:(c + 1) * chunk, :]) + bias)
        sv = jnp.concatenate(parts, axis=0)
        o_ref[...] += _dot((act * sv).astype(BF16), wo_ref[...])


def mixer_gmlp(x, g, w_uv, v_gain, w_s, b_s, w_out, *, tm):
    t, d = x.shape
    width = w_out.shape[0]
    tn = width // GMLP_GROUPS
    nd = GMLP_GROUPS
    chunk = GMLP_CHUNK
    assert tm % chunk == 0
    vg = v_gain.reshape(nd, 1, tn)
    bs = b_s.reshape(nd, chunk, 1)
    kern = functools.partial(_mixer_c_kernel, nd=nd, chunk=chunk)
    return pl.pallas_call(
        kern,
        grid=(t // tm, 2 * nd),
        in_specs=[_ANY,
                  pl.BlockSpec((1, d), lambda i, j: (0, 0)),
                  pl.BlockSpec((d, tn), lambda i, j: (0, jnp.where(j < nd, j + nd, j - nd))),
                  pl.BlockSpec((1, 1, tn), lambda i, j: (jnp.maximum(j - nd, 0), 0, 0)),
                  pl.BlockSpec((1, chunk, chunk), lambda i, j: (jnp.maximum(j - nd, 0), 0, 0)),
                  pl.BlockSpec((1, chunk, 1), lambda i, j: (jnp.maximum(j - nd, 0), 0, 0)),
                  pl.BlockSpec((tn, d), lambda i, j: (jnp.maximum(j - nd, 0), 0))],
        out_specs=pl.BlockSpec((tm, d), lambda i, j: (i, 0)),
        out_shape=jax.ShapeDtypeStruct((t, d), F32),
        scratch_shapes=[pltpu.VMEM((tm, d), BF16),
                        pltpu.VMEM((nd, tm, tn), F32),
                        pltpu.VMEM((tm, 1), F32)] + _XBUF(tm, d),
        compiler_params=_cparams(_SEQ2),
        name="mixer_gmlp",
    )(x, g, w_uv, vg, w_s, bs, w_out)


def _mixer_d_kernel(x_ref, xp_ref, g_ref, wa_ref, wg_ref, cw_ref, cb_ref, lg_ref, lb_ref,
                    wo_ref, o_ref, hn_ref, z_ref, mean_ref, stage_ref,
                    *, nd, halo, tiles_per_seq):
    i = pl.program_id(0)
    j = pl.program_id(1)
    tm = x_ref.shape[0]
    width = z_ref.shape[0] * z_ref.shape[2]

    @pl.when(j == 0)
    def _():
        x = x_ref[...]
        g = g_ref[...]
        hn_ref[pl.ds(halo, tm), :] = _rmsnorm_f32(x, g).astype(BF16)
        hn_ref[pl.ds(0, halo), :] = _rmsnorm_f32(xp_ref[...], g).astype(BF16)
        mean_ref[...] = jnp.zeros_like(mean_ref)

    def glu_into_stage(slot):
        hfull = hn_ref[...]
        z = _dot(hfull, wa_ref[...]) * jax.nn.sigmoid(_dot(hfull, wg_ref[...]))
        keep = _halo_keep_mask(halo + tm, halo, i % tiles_per_seq == 0)
        stage_ref[slot] = jnp.where(keep, z, 0.0)

    def conv_from_stage(slot, tile):
        kw = cw_ref.shape[0]
        rows = CONV_ROWS
        for r0 in range(0, tm, rows):
            rowsum = None
            for l0 in range(0, cw_ref.shape[1], V7X_LANES):
                lanes = slice(l0, l0 + V7X_LANES)
                win = stage_ref[slot, r0:r0 + halo + rows, lanes]
                acc = None
                for b in range(8):
                    wb = _shift_rows(win, b)
                    for s in range(b, kw, 8):
                        off = halo - (s - b)
                        term = wb[off:off + rows, :] * cw_ref[kw - 1 - s:kw - s, lanes]
                        acc = term if acc is None else acc + term
                acc = acc + cb_ref[0][:, lanes]
                z_ref[tile, r0:r0 + rows, lanes] = acc
                part = jnp.sum(acc, axis=-1, keepdims=True)
                rowsum = part if rowsum is None else rowsum + part
            mean_ref[r0:r0 + rows, :] += rowsum

    @pl.when(j == 0)
    def _():
        glu_into_stage(0)

    @pl.when(jnp.logical_and(j >= 1, j < nd))
    def _():
        conv_from_stage((j - 1) % 2, j - 1)
        glu_into_stage(j % 2)

    @pl.when(j == nd)
    def _():
        conv_from_stage((nd - 1) % 2, nd - 1)
        mean = mean_ref[...] * (1.0 / width)
        var = jnp.zeros_like(mean)
        for k in range(nd):
            c = z_ref[k] - mean
            var = var + jnp.sum(c * c, axis=-1, keepdims=True)
        rstd = lax.rsqrt(var * (1.0 / width) + NORM_EPS)
        tn = z_ref.shape[2]
        for k in range(nd):
            zn = ((z_ref[k] - mean) * rstd) * lg_ref[k] + lb_ref[k]
            hn_ref[pl.ds(halo, tm), k * tn:(k + 1) * tn] = (zn * jax.nn.sigmoid(zn)).astype(BF16)
        o_ref[...] = x_ref[...] + _dot(hn_ref[pl.ds(halo, tm), :], wo_ref[...])


def mixer_conformer(x, g, w_pw1, conv_w, conv_b, ln_g, ln_b, w_pw2, *, seq, tm, tn):
    t, d = x.shape
    nd = d // tn
    halo = HALO_D
    kw = conv_w.shape[0]
    assert kw - 1 <= halo and halo % 16 == 0 and tm % CONV_ROWS == 0 and tn % V7X_LANES == 0
    hb = tm // halo
    lo = lambda j: jnp.minimum(j, nd - 1)
    prev = lambda j: jnp.clip(j - 1, 0, nd - 1)
    kern = functools.partial(_mixer_d_kernel, nd=nd, halo=halo, tiles_per_seq=seq // tm)
    vec = lambda a: a.reshape(nd, 1, tn)
    whole = lambda shape: pl.BlockSpec(shape, lambda i, j: (0,) * len(shape))
    return pl.pallas_call(
        kern,
        grid=(t // tm, nd + 1),
        in_specs=[pl.BlockSpec((tm, d), lambda i, j: (i, 0)),
                  pl.BlockSpec((halo, d), lambda i, j: (jnp.maximum(i * hb - 1, 0), 0)),
                  pl.BlockSpec((1, d), lambda i, j: (0, 0)),
                  pl.BlockSpec((d, tn), lambda i, j: (0, lo(j))),
                  pl.BlockSpec((d, tn), lambda i, j: (0, nd + lo(j))),
                  pl.BlockSpec((kw, tn), lambda i, j: (0, prev(j))),
                  pl.BlockSpec((1, 1, tn), lambda i, j: (prev(j), 0, 0)),
                  whole((nd, 1, tn)),
                  whole((nd, 1, tn)),
                  pl.BlockSpec((d, d), lambda i, j: (0, 0), pipeline_mode=pl.Buffered(1))],
        out_specs=pl.BlockSpec((tm, d), lambda i, j: (i, 0)),
        out_shape=jax.ShapeDtypeStruct((t, d), F32),
        scratch_shapes=[pltpu.VMEM((halo + tm, d), BF16),
                        pltpu.VMEM((nd, tm, tn), F32),
                        pltpu.VMEM((tm, 1), F32),
                        pltpu.VMEM((2, halo + tm, tn), F32)],
        compiler_params=_cparams(("parallel", "arbitrary")),
        name="mixer_conformer",
    )(x, x, g, w_pw1, w_pw1, conv_w, vec(conv_b), vec(ln_g), vec(ln_b), w_pw2)


def _router_kernel(x_ref, g_ref, rt_ref, e1_ref, e2_ref, g1_ref, g2_ref, r1_ref, r2_ref,
                   cnt_ref, carry_ref, *, n_experts):
    tm = x_ref.shape[0]

    @pl.when(pl.program_id(0) == 0)
    def _():
        carry_ref[...] = jnp.zeros_like(carry_ref)

    h = _rmsnorm_f32(x_ref[...], g_ref[...])
    logits = [jnp.sum(h * rt_ref[e:e + 1, :], axis=-1, keepdims=True) for e in range(n_experts)]
    m1 = logits[0]
    i1 = jnp.zeros((tm, 1), jnp.int32)
    for e in range(1, n_experts):
        better = logits[e] > m1
        m1 = jnp.where(better, logits[e], m1)
        i1 = jnp.where(better, e, i1)
    m2 = jnp.full((tm, 1), -jnp.inf, F32)
    i2 = jnp.where(i1 == 0, 1, 0).astype(jnp.int32)
    for e in range(n_experts):
        better = jnp.logical_and(i1 != e, logits[e] > m2)
        m2 = jnp.where(better, logits[e], m2)
        i2 = jnp.where(better, e, i2)
    ex = jnp.exp(m2 - m1)
    denom = 1.0 + ex
    e1_ref[...] = i1
    e2_ref[...] = i2
    g1_ref[...] = 1.0 / denom
    g2_ref[...] = ex / denom

    lane = lax.broadcasted_iota(jnp.int32, (tm, V7X_LANES), 1)
    hit1 = lane == i1
    hit2 = lane == i2
    sel = jnp.logical_or(hit1, hit2).astype(BF16)
    row = lax.broadcasted_iota(jnp.int32, (tm, tm), 0)
    col = lax.broadcasted_iota(jnp.int32, (tm, tm), 1)
    before = (col < row).astype(BF16)
    ranks = _dot(before, sel) + carry_ref[...]
    r1_ref[...] = jnp.sum(jnp.where(hit1, ranks, 0.0), axis=-1, keepdims=True).astype(jnp.int32)
    r2_ref[...] = jnp.sum(jnp.where(hit2, ranks, 0.0), axis=-1, keepdims=True).astype(jnp.int32)
    total = carry_ref[...] + jnp.sum(sel.astype(F32), axis=0, keepdims=True)
    carry_ref[...] = total
    cnt_ref[...] = jnp.broadcast_to(total, cnt_ref.shape).astype(jnp.int32)


def moe_router(x, g, router_t, *, tm):
    t, d = x.shape
    n_experts = router_t.shape[0]
    col = lambda dt: jax.ShapeDtypeStruct((t, 1), dt)
    cspec = pl.BlockSpec((tm, 1), lambda i: (i, 0))
    return pl.pallas_call(
        functools.partial(_router_kernel, n_experts=n_experts),
        grid=(t // tm,),
        in_specs=[pl.BlockSpec((tm, d), lambda i: (i, 0)),
                  pl.BlockSpec((1, d), lambda i: (0, 0)),
                  pl.BlockSpec((n_experts, d), lambda i: (0, 0))],
        out_specs=[cspec, cspec, cspec, cspec, cspec, cspec,
                   pl.BlockSpec((8, V7X_LANES), lambda i: (0, 0))],
        out_shape=[col(jnp.int32), col(jnp.int32), col(F32), col(F32),
                   col(jnp.int32), col(jnp.int32),
                   jax.ShapeDtypeStruct((8, V7X_LANES), jnp.int32)],
        scratch_shapes=[pltpu.VMEM((1, V7X_LANES), F32)],
        compiler_params=_cparams(("arbitrary",)),
        name="moe_router",
    )(x, g, router_t)


def _row_copy_wait(src_ref, dst_ref, sem, n_rows_src):
    pltpu.make_async_copy(src_ref, dst_ref.at[pl.ds(0, n_rows_src)], sem).wait()


ISSUE_UNROLL = 8


def _dispatch_kernel(p1_ref, p2_ref, zf_ref, x_ref, g_ref, xs_ref, h_ref, sem, zsem, *, tm_moe):
    i = pl.program_id(0)
    tm = x_ref.shape[0]
    n_tiles = xs_ref.shape[0] // tm_moe

    @pl.when(i == 0)
    def _():
        h_ref[1] = jnp.zeros(h_ref.shape[1:], F32)

        def fill_copy(k, c):
            start = pl.multiple_of(k * tm_moe + c * tm, tm)
            return pltpu.make_async_copy(h_ref.at[1], xs_ref.at[pl.ds(start, tm)], zsem)

        def start_fill(k, carry):
            @pl.when(zf_ref[k] == 1)
            def _():
                for c in range(tm_moe // tm):
                    fill_copy(k, c).start()
            return carry

        def wait_fill(k, carry):
            @pl.when(zf_ref[k] == 1)
            def _():
                for c in range(tm_moe // tm):
                    fill_copy(k, c).wait()
            return carry

        lax.fori_loop(0, n_tiles, start_fill, 0)
        lax.fori_loop(0, n_tiles, wait_fill, 0)

    slot = i % 2
    h_ref[slot] = _rmsnorm_f32(x_ref[...], g_ref[...])

    def issue(r, carry):
        t = i * tm + r
        src = h_ref.at[slot, pl.ds(r, 1)]
        pltpu.make_async_copy(src, xs_ref.at[pl.ds(p1_ref[t], 1)], sem.at[slot]).start()
        pltpu.make_async_copy(src, xs_ref.at[pl.ds(p2_ref[t], 1)], sem.at[slot]).start()
        return carry

    lax.fori_loop(0, tm, issue, 0, unroll=ISSUE_UNROLL)

    def drain(s):
        for _ in range(TOP_K):
            _row_copy_wait(h_ref.at[s], xs_ref, sem.at[s], tm)

    @pl.when(i > 0)
    def _():
        drain(1 - slot)

    @pl.when(i == pl.num_programs(0) - 1)
    def _():
        drain(slot)


def moe_dispatch(x, g, pos1, pos2, zero_fill, n_rows, *, tm, tm_moe):
    t, d = x.shape
    assert tm_moe % tm == 0
    grid_spec = pltpu.PrefetchScalarGridSpec(
        num_scalar_prefetch=3,
        grid=(t // tm,),
        in_specs=[pl.BlockSpec((tm, d), lambda i, p1, p2, zf: (i, 0)),
                  pl.BlockSpec((1, d), lambda i, p1, p2, zf: (0, 0))],
        out_specs=_ANY,
        scratch_shapes=[pltpu.VMEM((2, tm, d), F32), pltpu.SemaphoreType.DMA((2,)),
                        pltpu.SemaphoreType.DMA(())],
    )
    return pl.pallas_call(
        functools.partial(_dispatch_kernel, tm_moe=tm_moe),
        grid_spec=grid_spec,
        out_shape=jax.ShapeDtypeStruct((n_rows, d), F32),
        compiler_params=_cparams(("arbitrary",)),
        name="moe_dispatch",
    )(pos1, pos2, zero_fill, x, g)


def _moe_ffn_kernel(te_ref, xs_hbm, w1_ref, w3_ref, w2_ref, ys_ref, hn_ref, xbuf, xsem,
                    *, n_tiles):
    i = pl.program_id(0)
    n_active = te_ref[n_tiles]
    active = i < n_active

    @pl.when(pl.program_id(1) == 0)
    def _():
        ys_ref[...] = jnp.zeros_like(ys_ref)

        @pl.when(active)
        def _():
            _await_tile(xs_hbm, xbuf, xsem, i)
            hn_ref[...] = xbuf[...].astype(BF16)
            _request_tile(xs_hbm, xbuf, xsem, i + 1, n_active)

    quarters = te_ref[n_tiles + 1 + i]

    def expert_rows(rows):
        hn = hn_ref[0:rows, :]
        a = _dot(hn, w1_ref[...])
        b = _dot(hn, w3_ref[...])
        act = (a * jax.nn.sigmoid(a) * b).astype(BF16)
        ys_ref[0:rows, :] += _dot(act, w2_ref[...])

    for q in range(1, TAIL_PARTS + 1):
        @pl.when(jnp.logical_and(active, quarters == q))
        def _(q=q):
            expert_rows(q * (hn_ref.shape[0] // TAIL_PARTS))


def moe_grouped_ffn(xs, tile_expert, w1, w3, w2, *, tm, tf):
    n_rows, d = xs.shape
    f = w1.shape[2]
    n_tiles = n_rows // tm
    nf = f // tf

    def jj(i, j, te):
        return jnp.where(i < te[n_tiles], j, nf - 1)

    grid_spec = pltpu.PrefetchScalarGridSpec(
        num_scalar_prefetch=1,
        grid=(n_tiles, nf),
        in_specs=[_ANY,
                  pl.BlockSpec((None, d, tf), lambda i, j, te: (te[i], 0, jj(i, j, te))),
                  pl.BlockSpec((None, d, tf), lambda i, j, te: (te[i], 0, jj(i, j, te))),
                  pl.BlockSpec((None, tf, d), lambda i, j, te: (te[i], jj(i, j, te), 0))],
        out_specs=pl.BlockSpec((tm, d), lambda i, j, te: (i, 0)),
        scratch_shapes=[pltpu.VMEM((tm, d), BF16)] + _XBUF(tm, d),
    )
    return pl.pallas_call(
        functools.partial(_moe_ffn_kernel, n_tiles=n_tiles),
        grid_spec=grid_spec,
        out_shape=jax.ShapeDtypeStruct((n_rows, d), F32),
        compiler_params=_cparams(("arbitrary", "arbitrary")),
        name="moe_grouped_ffn",
    )(tile_expert, xs, w1, w3, w2)


def _combine_kernel(p1_ref, p2_ref, x_ref, g1_ref, g2_ref, ys_ref, *rest, final_norm):
    if final_norm:
        fg_ref, o_ref, a_ref, b_ref, sem = rest
    else:
        o_ref, a_ref, b_ref, sem = rest
    i = pl.program_id(0)
    tm = x_ref.shape[0]

    def gather_tile(tile):
        slot = tile % 2

        def issue(r, carry):
            t = tile * tm + r
            pltpu.make_async_copy(ys_ref.at[pl.ds(p1_ref[t], 1)],
                                  a_ref.at[slot, pl.ds(r, 1)], sem.at[slot]).start()
            pltpu.make_async_copy(ys_ref.at[pl.ds(p2_ref[t], 1)],
                                  b_ref.at[slot, pl.ds(r, 1)], sem.at[slot]).start()
            return carry

        lax.fori_loop(0, tm, issue, 0, unroll=ISSUE_UNROLL)

    @pl.when(i == 0)
    def _():
        gather_tile(0)

    @pl.when(i + 1 < pl.num_programs(0))
    def _():
        gather_tile(i + 1)

    slot = i % 2
    _row_copy_wait(a_ref.at[slot], ys_ref, sem.at[slot], tm)
    _row_copy_wait(b_ref.at[slot], ys_ref, sem.at[slot], tm)
    y = x_ref[...] + (g1_ref[...] * a_ref[slot] + g2_ref[...] * b_ref[slot])
    if final_norm:
        y = _rmsnorm_f32(y, fg_ref[...])
    o_ref[...] = y


def moe_combine(x, g1, g2, ys, pos1, pos2, final_g, *, tm):
    t, d = x.shape
    final_norm = final_g is not None
    in_specs = [pl.BlockSpec((tm, d), lambda i, p1, p2: (i, 0)),
                pl.BlockSpec((tm, 1), lambda i, p1, p2: (i, 0)),
                pl.BlockSpec((tm, 1), lambda i, p1, p2: (i, 0)),
                pl.BlockSpec(memory_space=pl.ANY)]
    args = [pos1, pos2, x, g1, g2, ys]
    if final_norm:
        in_specs.append(pl.BlockSpec((1, d), lambda i, p1, p2: (0, 0)))
        args.append(final_g)
    grid_spec = pltpu.PrefetchScalarGridSpec(
        num_scalar_prefetch=2,
        grid=(t // tm,),
        in_specs=in_specs,
        out_specs=pl.BlockSpec((tm, d), lambda i, p1, p2: (i, 0)),
        scratch_shapes=[pltpu.VMEM((2, tm, d), F32), pltpu.VMEM((2, tm, d), F32),
                        pltpu.SemaphoreType.DMA((2,))],
    )
    return pl.pallas_call(
        functools.partial(_combine_kernel, final_norm=final_norm),
        grid_spec=grid_spec,
        out_shape=jax.ShapeDtypeStruct((t, d), F32),
        compiler_params=_cparams(("arbitrary",)),
        name="moe_combine",
    )(*args)


def moe_ffn(x, g, router, w1, w3, w2, final_g, *, tm_moe, tf):
    t, d = x.shape
    n_experts = router.shape[1]
    e1, e2, g1, g2, r1, r2, cnt = moe_router(x, g, router.T, tm=TM_ROW)

    counts = cnt[0, :n_experts]
    padded = ((counts + tm_moe - 1) // tm_moe) * tm_moe
    ends = jnp.cumsum(padded)
    starts = ends - padded
    pos1 = (starts[e1[:, 0]] + r1[:, 0]).astype(jnp.int32)
    pos2 = (starts[e2[:, 0]] + r2[:, 0]).astype(jnp.int32)
    n_rows = TOP_K * t + n_experts * tm_moe
    n_tiles = n_rows // tm_moe
    n_active = (ends[-1] // tm_moe).astype(jnp.int32)
    tile_row = jnp.arange(n_tiles, dtype=jnp.int32) * tm_moe
    tile_e = jnp.sum((tile_row[:, None] >= ends[None, :]).astype(jnp.int32), axis=1)
    tile_e = jnp.minimum(tile_e, n_experts - 1)
    last_e = tile_e[jnp.maximum(n_active - 1, 0)]
    tile_e = jnp.where(jnp.arange(n_tiles) < n_active, tile_e, last_e)
    valid_rows = (starts + counts)[tile_e] - tile_row
    part = tm_moe // TAIL_PARTS
    quarters = jnp.clip((valid_rows + part - 1) // part, 1, TAIL_PARTS)
    tile_expert = jnp.concatenate([tile_e, n_active[None], quarters]).astype(jnp.int32)
    group_tail = jnp.any((tile_row[:, None] + tm_moe == ends[None, :]) & (padded[None, :] > 0),
                         axis=1)
    zero_fill = (group_tail | (jnp.arange(n_tiles) >= n_active)).astype(jnp.int32)

    xs = moe_dispatch(x, g, pos1, pos2, zero_fill, n_rows, tm=TM_ROW, tm_moe=tm_moe)
    ys = moe_grouped_ffn(xs, tile_expert, w1, w3, w2, tm=tm_moe, tf=tf)
    return moe_combine(x, g1, g2, ys, pos1, pos2, final_g, tm=TM_ROW)


def kernel(x, mem, mem_norm, final_norm, mixer_norm, xattn_norm, ffn_norm, xa_wq, xa_wk, xa_wv, xa_wo, a_w_in, a_conv, a_w_out, b_w_map, b_scale, c_w_uv, c_v_norm, c_w_s, c_b_s, c_w_out, d_w_pw1, d_conv, d_conv_b, d_ln_g, d_ln_b, d_w_pw2, f_w1, f_w3, f_w2, m_router, m_w1, m_w3, m_w2):
    batch, seq, d = x.shape
    depth = mixer_norm.shape[0]
    n_mixers = 4
    row = lambda v: v.reshape(1, -1)

    def bf(w, idx):
        per_layer = 1
        for s in w.shape[1:-2]:
            per_layer *= s
        out = cast_bf16(w, idx * per_layer, per_layer)
        return out.reshape(w.shape[1:])

    mem2 = mem.reshape(batch * mem.shape[1], d)
    k_all = rms_matmul_layers(mem2, row(mem_norm), cast_bf16(xa_wk), tm=mem2.shape[0], tn=TN)
    v_all = rms_matmul_layers(mem2, row(mem_norm), cast_bf16(xa_wv), tm=mem2.shape[0], tn=TN)
    k_all = k_all.reshape(batch, mem.shape[1], depth * d)
    v_all = v_all.reshape(batch, mem.shape[1], depth * d)
    wq_all = cast_bf16(xa_wq)
    wo_all = cast_bf16(xa_wo)

    xt = x.reshape(batch * seq, d)
    for i in range(depth):
        mixer, j = i % n_mixers, i // n_mixers
        g = row(mixer_norm[i])
        if mixer == 0:
            xt = mixer_short_conv(xt, g, bf(a_w_in, j), a_conv[j], bf(a_w_out, j),
                                  seq=seq, tm=TM, tn=TN)
        elif mixer == 1:
            xt = mixer_pooling(xt, g, bf(b_w_map, j), row(b_scale[j]), seq=seq, tm=TM_VPU)
        elif mixer == 2:
            xt = mixer_gmlp(xt, g, bf(c_w_uv, j), c_v_norm[j], c_w_s[j], c_b_s[j],
                            bf(c_w_out, j), tm=TM)
        else:
            xt = mixer_conformer(xt, g, bf(d_w_pw1, j), d_conv[j], d_conv_b[j], d_ln_g[j],
                                 d_ln_b[j], bf(d_w_pw2, j), seq=seq, tm=TM_VPU, tn=TN)
        xt = cross_attention(xt, row(xattn_norm[i]), wq_all, k_all, v_all, i, wo_all,
                             seq=seq, tm=TM_XATTN)
        k = i // 2
        gf = row(ffn_norm[i])
        if i % 2 == 0:
            xt = swiglu_ffn(xt, gf, bf(f_w1, k), bf(f_w3, k), bf(f_w2, k), tm=TM, tf=TN)
        else:
            fin = row(final_norm) if i == depth - 1 else None
            xt = moe_ffn(xt, gf, m_router[k], bf(m_w1, k), bf(m_w3, k), bf(m_w2, k), fin,
                         tm_moe=TM_MOE, tf=TN)
    if depth % 2 == 1:
        raise NotImplementedError("final RMSNorm is fused into the last (expert) layer")
    return xt.reshape(batch, seq, d)
```
